```python
import jax, jax.numpy as jnp
from jax import lax
import numpy as np

D_MODEL = 1024
BATCH = 8
SEQ = 4096
DEPTH = 1

N_MEM = 256
A_HEADS = 8
HEAD_DIM = 64
A_WIDTH = A_HEADS * HEAD_DIM
IDX_HEADS = 4
IDX_DIM = 64
TOPK_MAX = 256
Q_BLOCK = 128
B_GROUPS = 8
B_WIDTH = D_MODEL - A_WIDTH
CONV_B_WIDTH = 31
MIX_WIDTH = A_WIDTH + B_WIDTH
ROPE_THETA = 500000.0
ROT_DIM = HEAD_DIM // 4
CROSS_HEADS = 4
CROSS_HEAD_DIM = D_MODEL // CROSS_HEADS
D_FF = 2816
FFN_CONV_WIDTH = 3
EPS = 1e-6
IN_SPLITS = (A_WIDTH, HEAD_DIM, HEAD_DIM, IDX_HEADS * IDX_DIM, IDX_DIM, IDX_HEADS, 2 * B_WIDTH)
IN_COLS = sum(IN_SPLITS)

kernel_name = 'hybrid_dsa_conformer_parallel_block'


def rmsnorm(x, g):
    xf = x.astype(jnp.float32)
    y = xf * lax.rsqrt(jnp.mean(xf * xf, axis=-1, keepdims=True) + EPS)
    return (y * g.astype(jnp.float32)).astype(x.dtype)


def layernorm(x, g, b):
    xf = x.astype(jnp.float32)
    mu = jnp.mean(xf, axis=-1, keepdims=True)
    var = jnp.mean(jnp.square(xf - mu), axis=-1, keepdims=True)
    y = (xf - mu) * lax.rsqrt(var + EPS)
    return (y * g.astype(jnp.float32) + b.astype(jnp.float32)).astype(x.dtype)


def causal_dwconv(x, w, b):
    width, ch = w.shape
    y = lax.conv_general_dilated(
        x, w.astype(x.dtype)[:, None, :], window_strides=(1,),
        padding=[(width - 1, 0)], dimension_numbers=('NWC', 'WIO', 'NWC'),
        feature_group_count=ch)
    return y + b.astype(x.dtype)


def rope_tables(positions):
    freqs = ROPE_THETA ** (-jnp.arange(0, ROT_DIM, 2, dtype=jnp.float32) / ROT_DIM)
    ang = positions.astype(jnp.float32)[:, :, None, None] * freqs
    return jnp.cos(ang), jnp.sin(ang)


def partial_rope(x, cos, sin):
    xr = x[..., :ROT_DIM].astype(jnp.float32)
    x1, x2 = xr[..., :ROT_DIM // 2], xr[..., ROT_DIM // 2:]
    rot = jnp.concatenate([x1 * cos - x2 * sin, x2 * cos + x1 * sin], axis=-1)
    return jnp.concatenate([rot.astype(x.dtype), x[..., ROT_DIM:]], axis=-1)


def dsa_attention(q, k, v, qi, ki, wi):
    bsz, seq = q.shape[0], q.shape[1]
    nb = seq // Q_BLOCK
    topk = min(TOPK_MAX, seq // 4)
    qi = qi * (IDX_DIM ** -0.5)
    wi = wi * (IDX_HEADS ** -0.5)
    gather = jax.vmap(lambda a, i: a[i])
    key_pos = jnp.arange(seq)

    def block(args):
        bi, qb, qib, wib = args
        t = bi * Q_BLOCK + jnp.arange(Q_BLOCK)
        causal = key_pos[None, :] <= t[:, None]
        rel = jax.nn.relu(jnp.einsum('bqhd,bsd->bqhs', qib, ki).astype(jnp.float32))
        score = jnp.einsum('bqh,bqhs->bqs', wib.astype(jnp.float32), rel)
        score = jnp.where(causal[None], score, -jnp.inf)
        _, sel = lax.top_k(score, topk)
        k_sel = gather(k, sel)
        v_sel = gather(v, sel)
        valid = sel <= t[None, :, None]
        logits = jnp.einsum('bqhd,bqkd->bqhk', qb, k_sel).astype(jnp.float32) * (HEAD_DIM ** -0.5)
        logits = jnp.where(valid[:, :, None, :], logits, -jnp.inf)
        p = jax.nn.softmax(logits, axis=-1).astype(v.dtype)
        return jnp.einsum('bqhk,bqkd->bqhd', p, v_sel)

    to_blocks = lambda a: a.reshape(bsz, nb, Q_BLOCK, *a.shape[2:]).swapaxes(0, 1)
    out = lax.map(block, (jnp.arange(nb), to_blocks(q), to_blocks(qi), to_blocks(wi)))
    return out.swapaxes(0, 1).reshape(bsz, seq, A_WIDTH)


def hybrid_layer(x, mem, cos, sin, norm_mix_g, w_in, w_out, conv_b_w, conv_b_b, ln_b_g, ln_b_b,
                 norm_cross_g, norm_mem_g, w_q_cross, w_k_cross, w_v_cross, w_o_cross,
                 norm_ffn_g, w_gate, w_up, ffn_conv_w, ffn_conv_b, w_down):
    bsz, seq, _ = x.shape
    h = rmsnorm(x, norm_mix_g)
    proj = h @ w_in
    cuts = [int(c) for c in np.cumsum(IN_SPLITS)[:-1]]
    q, k, v, qi, ki, wi, glu = jnp.split(proj, cuts, axis=-1)
    q = partial_rope(q.reshape(bsz, seq, A_HEADS, HEAD_DIM), cos, sin)
    k = partial_rope(k[:, :, None, :], cos, sin)[:, :, 0]
    qi = partial_rope(qi.reshape(bsz, seq, IDX_HEADS, IDX_DIM), cos, sin)
    ki = partial_rope(ki[:, :, None, :], cos, sin)[:, :, 0]
    a_out = dsa_attention(q, k, v, qi, ki, wi)
    ga, gg = jnp.split(glu, 2, axis=-1)
    u = ga * jax.nn.sigmoid(gg)
    u = causal_dwconv(u, conv_b_w, conv_b_b)
    b_out = jax.nn.silu(layernorm(u, ln_b_g, ln_b_b))
    x = x + jnp.concatenate([a_out, b_out], axis=-1) @ w_out
    hq = rmsnorm(x, norm_cross_g)
    m = rmsnorm(mem, norm_mem_g)
    qc = (hq @ w_q_cross).reshape(bsz, seq, CROSS_HEADS, CROSS_HEAD_DIM)
    kc = (m @ w_k_cross).reshape(bsz, -1, CROSS_HEADS, CROSS_HEAD_DIM)
    vc = (m @ w_v_cross).reshape(bsz, -1, CROSS_HEADS, CROSS_HEAD_DIM)
    logits = jnp.einsum('bshd,bnhd->bhsn', qc, kc).astype(jnp.float32) * (CROSS_HEAD_DIM ** -0.5)
    p = jax.nn.softmax(logits, axis=-1).astype(vc.dtype)
    oc = jnp.einsum('bhsn,bnhd->bshd', p, vc).reshape(bsz, seq, D_MODEL)
    x = x + oc @ w_o_cross
    hf = rmsnorm(x, norm_ffn_g)
    g = causal_dwconv(hf @ w_gate, ffn_conv_w, ffn_conv_b)
    x = x + (jax.nn.silu(g) * (hf @ w_up)) @ w_down
    return x


def setup_inputs(seed: int = 0) -> dict:
    key = jax.random.key(seed)
    ks = jax.random.split(key, 24)
    f32 = jnp.float32
    nrm = lambda k, shape, scale: jax.random.normal(k, shape, f32) * scale
    gain = lambda k, shape: 1.0 + 0.02 * jax.random.normal(k, shape, f32)
    L = DEPTH
    offsets = jax.random.randint(ks[2], (BATCH, 1), 0, 2048, dtype=jnp.int32)
    positions = offsets + jnp.arange(SEQ, dtype=jnp.int32)[None, :]
    return {
        'x': nrm(ks[0], (BATCH, SEQ, D_MODEL), 1.0),
        'mem': nrm(ks[1], (BATCH, N_MEM, D_MODEL), 1.0),
        'positions': positions,
        'norm_mix_g': gain(ks[3], (L, D_MODEL)),
        'w_in': nrm(ks[4], (L, D_MODEL, IN_COLS), D_MODEL ** -0.5),
        'w_out': nrm(ks[5], (L, MIX_WIDTH, D_MODEL), MIX_WIDTH ** -0.5),
        'conv_b_w': nrm(ks[6], (L, CONV_B_WIDTH, B_WIDTH), CONV_B_WIDTH ** -0.5),
        'conv_b_b': nrm(ks[7], (L, B_WIDTH), 0.02),
        'ln_b_g': gain(ks[8], (L, B_WIDTH)),
        'ln_b_b': nrm(ks[9], (L, B_WIDTH), 0.02),
        'norm_cross_g': gain(ks[10], (L, D_MODEL)),
        'norm_mem_g': gain(ks[11], (L, D_MODEL)),
        'w_q_cross': nrm(ks[12], (L, D_MODEL, D_MODEL), D_MODEL ** -0.5),
        'w_k_cross': nrm(ks[13], (L, D_MODEL, D_MODEL), D_MODEL ** -0.5),
        'w_v_cross': nrm(ks[14], (L, D_MODEL, D_MODEL), D_MODEL ** -0.5),
        'w_o_cross': nrm(ks[15], (L, D_MODEL, D_MODEL), D_MODEL ** -0.5),
        'norm_ffn_g': gain(ks[16], (L, D_MODEL)),
        'w_gate': nrm(ks[17], (L, D_MODEL, D_FF), D_MODEL ** -0.5),
        'w_up': nrm(ks[18], (L, D_MODEL, D_FF), D_MODEL ** -0.5),
        'ffn_conv_w': nrm(ks[19], (L, FFN_CONV_WIDTH, D_FF), FFN_CONV_WIDTH ** -0.5),
        'ffn_conv_b': nrm(ks[20], (L, D_FF), 0.02),
        'w_down': nrm(ks[21], (L, D_FF, D_MODEL), D_FF ** -0.5),
        'norm_final_g': gain(ks[22], (D_MODEL,)),
    }


def reference(x, mem, positions, norm_mix_g, w_in, w_out, conv_b_w, conv_b_b, ln_b_g, ln_b_b,
              norm_cross_g, norm_mem_g, w_q_cross, w_k_cross, w_v_cross, w_o_cross,
              norm_ffn_g, w_gate, w_up, ffn_conv_w, ffn_conv_b, w_down, norm_final_g):
    cos, sin = rope_tables(positions)
    for l in range(DEPTH):
        x = hybrid_layer(x, mem, cos, sin, norm_mix_g[l], w_in[l], w_out[l], conv_b_w[l], conv_b_b[l],
                         ln_b_g[l], ln_b_b[l], norm_cross_g[l], norm_mem_g[l], w_q_cross[l], w_k_cross[l],
                         w_v_cross[l], w_o_cross[l], norm_ffn_g[l], w_gate[l], w_up[l], ffn_conv_w[l],
                         ffn_conv_b[l], w_down[l])
    return rmsnorm(x, norm_final_g)
```

```python
import functools

import jax
import jax.numpy as jnp
from jax import lax
from jax.experimental import pallas as pl
from jax.experimental.pallas import tpu as pltpu

F32 = jnp.float32
BF16 = jnp.bfloat16

D_MODEL = 1024
A_HEADS = 8
HEAD_DIM = 64
A_WIDTH = A_HEADS * HEAD_DIM
IDX_HEADS = 4
IDX_DIM = 64
TOPK_MAX = 256
B_WIDTH = D_MODEL - A_WIDTH
CONV_B_WIDTH = 31
ROPE_THETA = 500000.0
ROT_DIM = HEAD_DIM // 4
ROT_HALF = ROT_DIM // 2
CROSS_HEADS = 4
CROSS_HEAD_DIM = D_MODEL // CROSS_HEADS
D_FF = 2816
FFN_CONV_WIDTH = 3
EPS = 1e-6

SUBLANES = 8
TOKEN_TILE = 512
QUERY_TILE = 256
CONV_HALO = 32
CONV_ROWS = 32
FF_CHUNK = 256
VMEM_LIMIT_BYTES = 56 * 1024 * 1024

ROW_Q = 0
ROW_QI = ROW_Q + A_WIDTH
ROW_K = ROW_QI + IDX_HEADS * IDX_DIM
ROW_KI = ROW_K + HEAD_DIM
ROW_V = ROW_KI + IDX_DIM
ROW_WI = ROW_V + HEAD_DIM
ROWS_T = ROW_WI + 16

INT_MIN = -(2 ** 31)
KEY_NEG_INF = INT_MIN + 0x7FFFFF
KEY_POS_INF = 0x7F800000
NEG_INF = float("-inf")


def _rmsnorm(x, g):
    ms = jnp.mean(x * x, axis=-1, keepdims=True)
    return (x * lax.rsqrt(ms + EPS)) * g


def _dot(a, b):
    return jnp.dot(a, b, preferred_element_type=F32)


def _dot_nt(a, b):
    return lax.dot_general(a, b, (((1,), (1,)), ((), ())), preferred_element_type=F32)


def _in_proj_kernel(tiles_per_seq, pos_ref, x_ref, g_ref, wt_ref, wg_ref, freq_ref, cw_ref, cb_ref,
                    lng_ref, lnb_ref, qT_ref, qiT_ref, k_ref, ki_ref, vT_ref, wiT_ref, b_ref, ubuf):
    tm = x_ref.shape[0]
    h = _rmsnorm(x_ref[...], g_ref[...]).astype(BF16)

    r = _dot_nt(wt_ref[...], h)
    ang = freq_ref[...] * pos_ref[...].astype(F32)
    cos = jnp.cos(ang)
    sin = jnp.sin(ang)

    def rope(xt, heads):
        x3 = xt.reshape(heads, HEAD_DIM, tm)
        x1 = x3[:, 0:ROT_HALF, :]
        x2 = x3[:, ROT_HALF:ROT_DIM, :]
        out = jnp.concatenate([x1 * cos - x2 * sin, x2 * cos + x1 * sin, x3[:, ROT_DIM:, :]], axis=1)
        return out.reshape(heads * HEAD_DIM, tm)

    qT_ref[...] = (rope(r[ROW_Q:ROW_QI], A_HEADS) * (HEAD_DIM ** -0.5)).astype(BF16)
    qiT_ref[...] = (rope(r[ROW_QI:ROW_K], IDX_HEADS) * (IDX_DIM ** -0.5)).astype(BF16)
    kk = rope(r[ROW_K:ROW_V], 2).T
    k_ref[...] = kk[:, :HEAD_DIM].astype(BF16)
    ki_ref[...] = kk[:, HEAD_DIM:].astype(BF16)
    vT_ref[...] = r[ROW_V:ROW_WI].astype(BF16)
    wiT_ref[...] = r[ROW_WI:ROW_WI + SUBLANES] * (IDX_HEADS ** -0.5)

    glu = _dot(h, wg_ref[...])
    u = glu[:, :B_WIDTH] * jax.nn.sigmoid(glu[:, B_WIDTH:])

    @pl.when(pl.program_id(0) % tiles_per_seq == 0)
    def _():
        ubuf[0:CONV_HALO, :] = jnp.zeros((CONV_HALO, B_WIDTH), F32)

    ubuf[CONV_HALO:CONV_HALO + tm, :] = u
    base = CONV_HALO - (CONV_B_WIDTH - 1)
    for c in range(tm // CONV_ROWS):
        acc = jnp.broadcast_to(cb_ref[...], (CONV_ROWS, B_WIDTH))
        for j in range(CONV_B_WIDTH):
            acc = acc + ubuf[c * CONV_ROWS + base + j:c * CONV_ROWS + base + j + CONV_ROWS, :] * cw_ref[j:j + 1, :]
        mu = jnp.mean(acc, axis=-1, keepdims=True)
        var = jnp.mean(jnp.square(acc - mu), axis=-1, keepdims=True)
        y = (acc - mu) * lax.rsqrt(var + EPS) * lng_ref[...] + lnb_ref[...]
        b_ref[c * CONV_ROWS:(c + 1) * CONV_ROWS, :] = (y * jax.nn.sigmoid(y)).astype(BF16)
    ubuf[0:CONV_HALO, :] = ubuf[tm:tm + CONV_HALO, :]


def _in_proj(pos, x2d, g, wt, wg, freq, cw, cb, lng, lnb, seq):
    T = x2d.shape[0]
    tm = TOKEN_TILE
    full = lambda shape: pl.BlockSpec(shape, lambda i: (0,) * len(shape))
    return pl.pallas_call(
        functools.partial(_in_proj_kernel, seq // tm),
        grid=(T // tm,),
        in_specs=[
            pl.BlockSpec((1, tm), lambda i: (0, i)),
            pl.BlockSpec((tm, D_MODEL), lambda i: (i, 0)),
            full((1, D_MODEL)),
            full((ROWS_T, D_MODEL)),
            full((D_MODEL, 2 * B_WIDTH)),
            full((ROT_HALF, 1)),
            full((CONV_B_WIDTH, B_WIDTH)),
            full((1, B_WIDTH)),
            full((1, B_WIDTH)),
            full((1, B_WIDTH)),
        ],
        out_specs=[
            pl.BlockSpec((A_WIDTH, tm), lambda i: (0, i)),
            pl.BlockSpec((IDX_HEADS * IDX_DIM, tm), lambda i: (0, i)),
            pl.BlockSpec((tm, HEAD_DIM), lambda i: (i, 0)),
            pl.BlockSpec((tm, IDX_DIM), lambda i: (i, 0)),
            pl.BlockSpec((HEAD_DIM, tm), lambda i: (0, i)),
            pl.BlockSpec((SUBLANES, tm), lambda i: (0, i)),
            pl.BlockSpec((tm, B_WIDTH), lambda i: (i, 0)),
        ],
        out_shape=[
            jax.ShapeDtypeStruct((A_WIDTH, T), BF16),
            jax.ShapeDtypeStruct((IDX_HEADS * IDX_DIM, T), BF16),
            jax.ShapeDtypeStruct((T, HEAD_DIM), BF16),
            jax.ShapeDtypeStruct((T, IDX_DIM), BF16),
            jax.ShapeDtypeStruct((HEAD_DIM, T), BF16),
            jax.ShapeDtypeStruct((SUBLANES, T), F32),
            jax.ShapeDtypeStruct((T, B_WIDTH), BF16),
        ],
        scratch_shapes=[pltpu.VMEM((CONV_HALO + tm, B_WIDTH), F32)],
        compiler_params=pltpu.CompilerParams(
            dimension_semantics=("arbitrary",), vmem_limit_bytes=VMEM_LIMIT_BYTES),
        name="in_proj",
    )(pos, x2d, g, wt, wg, freq, cw, cb, lng, lnb)


def _key_to_f32(key):
    bits = jnp.where(key >= 0, key, key ^ jnp.int32(0x7FFFFFFF))
    return lax.bitcast_convert_type(bits, F32)


def _dsa_kernel(qT_ref, qiT_ref, wT_ref, k_ref, ki_ref, vT_ref, o_ref, sc_ref, acc_ref, m_ref, l_ref):
    tq = qT_ref.shape[1]
    kc = tq
    qi_blk = pl.program_id(1)
    nchunk = qi_blk + 1

    row = lax.broadcasted_iota(jnp.int32, (kc, tq), 0)
    col = lax.broadcasted_iota(jnp.int32, (kc, tq), 1)
    causal = row <= col

    def chunk_start(c):
        return pl.multiple_of(c * kc, kc)

    w = wT_ref[...]

    def score_body(c, carry):
        r0 = chunk_start(c)
        ki_c = ki_ref[pl.ds(r0, kc), :]
        s = jnp.zeros((kc, tq), F32)
        for h in range(IDX_HEADS):
            d = _dot(ki_c, qiT_ref[h * IDX_DIM:(h + 1) * IDX_DIM, :])
            s = s + w[h:h + 1, :] * jnp.maximum(d, 0.0)
        sc_ref[pl.ds(r0, kc), :] = s
        return carry

    lax.fori_loop(0, nchunk, score_body, 0)
    d0 = chunk_start(qi_blk)
    sc_ref[pl.ds(d0, kc), :] = jnp.where(causal, sc_ref[pl.ds(d0, kc), :], NEG_INF)

    def count_ge(thr):
        def body(c, acc):
            blk = sc_ref[pl.ds(chunk_start(c), kc), :]
            ge = (blk >= thr).astype(jnp.int32)
            return acc + ge.reshape(kc // 32, 32, tq).sum(axis=0)
        acc = lax.fori_loop(0, nchunk, body, jnp.zeros((32, tq), jnp.int32))
        return acc.sum(axis=0, keepdims=True)

    def bit_body(it, t):
        trial = t + lax.shift_left(jnp.int32(1), 31 - it)
        cnt = count_ge(_key_to_f32(trial))
        return jnp.where(cnt >= TOPK_MAX, trial, t)

    t = lax.fori_loop(0, 32, bit_body, jnp.full((1, tq), INT_MIN, jnp.int32))
    t = jnp.maximum(t, KEY_NEG_INF)
    upper = _key_to_f32(t + 1)
    no_upper = t >= KEY_POS_INF

    def thr_body(c, carry):
        bmax, cnt = carry
        blk = sc_ref[pl.ds(chunk_start(c), kc), :]
        below = (blk < upper) | no_upper
        bmax = jnp.maximum(bmax, jnp.where(below, blk, NEG_INF).reshape(kc // 32, 32, tq).max(axis=0))
        cnt = cnt + jnp.where(below, 0, 1).reshape(kc // 32, 32, tq).sum(axis=0)
        return bmax, cnt

    bmax, cnt = lax.fori_loop(
        0, nchunk, thr_body,
        (jnp.full((32, tq), NEG_INF, F32), jnp.zeros((32, tq), jnp.int32)))
    tau = bmax.max(axis=0, keepdims=True)
    need = (TOPK_MAX - cnt.sum(axis=0, keepdims=True)).astype(F32)

    lower_tri = (row > col).astype(BF16)

    def bias_body(c, carry):
        r0 = chunk_start(c)
        blk = sc_ref[pl.ds(r0, kc), :]
        eq = blk == tau
        eqf = jnp.where(eq, 1.0, 0.0)
        rank = _dot(lower_tri, eqf.astype(BF16)) + carry
        sel = (blk > tau) | (eq & (rank < need))
        sc_ref[pl.ds(r0, kc), :] = jnp.where(sel, 0.0, NEG_INF)
        return carry + eqf.sum(axis=0, keepdims=True)

    lax.fori_loop(0, nchunk, bias_body, jnp.zeros((1, tq), F32))
    sc_ref[pl.ds(d0, kc), :] = jnp.where(causal, sc_ref[pl.ds(d0, kc), :], NEG_INF)

    m_ref[...] = jnp.full(m_ref.shape, NEG_INF, F32)
    l_ref[...] = jnp.zeros(l_ref.shape, F32)
    acc_ref[...] = jnp.zeros(acc_ref.shape, F32)

    def att_body(c, carry):
        r0 = chunk_start(c)
        k_c = k_ref[pl.ds(r0, kc), :]
        vT_c = vT_ref[:, pl.ds(r0, kc)]
        bias = sc_ref[pl.ds(r0, kc), :]
        for h in range(A_HEADS):
            hs = slice(h * HEAD_DIM, (h + 1) * HEAD_DIM)
            logit = _dot(k_c, qT_ref[hs, :]) + bias
            m_old = m_ref[h:h + 1, :]
            m_new = jnp.maximum(m_old, logit.max(axis=0, keepdims=True))
            m_safe = jnp.where(m_new == NEG_INF, 0.0, m_new)
            p = jnp.exp(logit - m_safe)
            alpha = jnp.exp(m_old - m_safe)
            l_ref[h:h + 1, :] = alpha * l_ref[h:h + 1, :] + p.sum(axis=0, keepdims=True)
            acc_ref[hs, :] = alpha * acc_ref[hs, :] + _dot(vT_c, p.astype(BF16))
            m_ref[h:h + 1, :] = m_new
        return carry

    lax.fori_loop(0, nchunk, att_body, 0)

    inv = 1.0 / l_ref[...]
    for h in range(A_HEADS):
        hs = slice(h * HEAD_DIM, (h + 1) * HEAD_DIM)
        acc_ref[hs, :] = acc_ref[hs, :] * inv[h:h + 1, :]
    o_ref[...] = acc_ref[...].T.astype(BF16)


def _dsa(qT, qiT, wiT, k, ki, vT, batch, seq):
    tq = QUERY_TILE
    nq = seq // tq
    T = batch * seq
    return pl.pallas_call(
        _dsa_kernel,
        grid=(batch, nq),
        in_specs=[
            pl.BlockSpec((A_WIDTH, tq), lambda b, i: (0, b * nq + i)),
            pl.BlockSpec((IDX_HEADS * IDX_DIM, tq), lambda b, i: (0, b * nq + i)),
            pl.BlockSpec((SUBLANES, tq), lambda b, i: (0, b * nq + i)),
            pl.BlockSpec((seq, HEAD_DIM), lambda b, i: (b, 0)),
            pl.BlockSpec((seq, IDX_DIM), lambda b, i: (b, 0)),
            pl.BlockSpec((HEAD_DIM, seq), lambda b, i: (0, b)),
        ],
        out_specs=pl.BlockSpec((tq, A_WIDTH), lambda b, i: (b * nq + i, 0)),
        out_shape=jax.ShapeDtypeStruct((T, A_WIDTH), BF16),
        scratch_shapes=[
            pltpu.VMEM((seq, tq), F32),
            pltpu.VMEM((A_WIDTH, tq), F32),
            pltpu.VMEM((A_HEADS, tq), F32),
            pltpu.VMEM((A_HEADS, tq), F32),
        ],
        compiler_params=pltpu.CompilerParams(
            dimension_semantics=("arbitrary", "arbitrary"), vmem_limit_bytes=VMEM_LIMIT_BYTES),
        name="dsa",
    )(qT, qiT, wiT, k, ki, vT)


def _mem_kv_kernel(mem_ref, g_ref, wk_ref, wv_ref, kc_ref, vc_ref):
    m = _rmsnorm(mem_ref[...], g_ref[...]).astype(BF16)
    kc_ref[...] = _dot(m, wk_ref[...]).astype(BF16)
    vc_ref[...] = _dot(m, wv_ref[...]).astype(BF16)


def _mem_kv(mem2d, g, wk, wv):
    M = mem2d.shape[0]
    tm = TOKEN_TILE
    full = lambda shape: pl.BlockSpec(shape, lambda i: (0,) * len(shape))
    return pl.pallas_call(
        _mem_kv_kernel,
        grid=(M // tm,),
        in_specs=[pl.BlockSpec((tm, D_MODEL), lambda i: (i, 0)), full((1, D_MODEL)),
                  full((D_MODEL, D_MODEL)), full((D_MODEL, D_MODEL))],
        out_specs=[pl.BlockSpec((tm, D_MODEL), lambda i: (i, 0))] * 2,
        out_shape=[jax.ShapeDtypeStruct((M, D_MODEL), BF16)] * 2,
        compiler_params=pltpu.CompilerParams(
            dimension_semantics=("arbitrary",), vmem_limit_bytes=VMEM_LIMIT_BYTES),
        name="mem_kv",
    )(mem2d, g, wk, wv)


def _mid_kernel(x_ref, a_ref, b_ref, wa_ref, wb_ref, gc_ref, wq_ref, kc_ref, vc_ref, wo_ref, gf_ref,
                x2_ref, hf_ref):
    x1 = x_ref[...] + _dot(a_ref[...], wa_ref[...]) + _dot(b_ref[...], wb_ref[...])
    hq = _rmsnorm(x1, gc_ref[...]).astype(BF16)
    qc = (_dot(hq, wq_ref[...]) * (CROSS_HEAD_DIM ** -0.5)).astype(BF16)
    heads = []
    for h in range(CROSS_HEADS):
        hs = slice(h * CROSS_HEAD_DIM, (h + 1) * CROSS_HEAD_DIM)
        logit = _dot_nt(qc[:, hs], kc_ref[:, hs])
        p = jnp.exp(logit - logit.max(axis=-1, keepdims=True))
        p = p * (1.0 / p.sum(axis=-1, keepdims=True))
        heads.append(_dot(p.astype(BF16), vc_ref[:, hs]))
    oc = jnp.concatenate(heads, axis=-1).astype(BF16)
    x2 = x1 + _dot(oc, wo_ref[...])
    x2_ref[...] = x2
    hf_ref[...] = _rmsnorm(x2, gf_ref[...]).astype(BF16)


def _mid(x2d, a, b, wa, wb, gc, wq, kc, vc, wo, gf, seq, n_mem):
    T = x2d.shape[0]
    tm = TOKEN_TILE
    tiles_per_seq = seq // tm
    full = lambda shape: pl.BlockSpec(shape, lambda i: (0,) * len(shape))
    tok = lambda width: pl.BlockSpec((tm, width), lambda i: (i, 0))
    mem = pl.BlockSpec((n_mem, D_MODEL), lambda i: (i // tiles_per_seq, 0))
    return pl.pallas_call(
        _mid_kernel,
        grid=(T // tm,),
        in_specs=[tok(D_MODEL), tok(A_WIDTH), tok(B_WIDTH), full((A_WIDTH, D_MODEL)), full((B_WIDTH, D_MODEL)),
                  full((1, D_MODEL)), full((D_MODEL, D_MODEL)), mem, mem, full((D_MODEL, D_MODEL)),
                  full((1, D_MODEL))],
        out_specs=[tok(D_MODEL), tok(D_MODEL)],
        out_shape=[jax.ShapeDtypeStruct((T, D_MODEL), F32), jax.ShapeDtypeStruct((T, D_MODEL), BF16)],
        compiler_params=pltpu.CompilerParams(
            dimension_semantics=("arbitrary",), vmem_limit_bytes=VMEM_LIMIT_BYTES),
        name="mid",
    )(x2d, a, b, wa, wb, gc, wq, kc, vc, wo, gf)


def _ffn_kernel(tiles_per_seq, x_ref, h_ref, wg_ref, wu_ref, cw_ref, cb_ref, wd_ref, gfin_ref, y_ref,
                halo, gbuf, acc_ref):
    tm = x_ref.shape[0]

    @pl.when(pl.program_id(0) % tiles_per_seq == 0)
    def _():
        halo[...] = jnp.zeros(halo.shape, F32)

    h = h_ref[...]
    acc_ref[...] = x_ref[...]
    for c in range(D_FF // FF_CHUNK):
        cs = slice(c * FF_CHUNK, (c + 1) * FF_CHUNK)
        gate = _dot(h, wg_ref[:, cs])
        up = _dot(h, wu_ref[:, cs])
        gbuf[0:SUBLANES, :] = halo[:, cs]
        gbuf[SUBLANES:SUBLANES + tm, :] = gate
        halo[:, cs] = gate[tm - SUBLANES:, :]
        cw = cw_ref[:, cs]
        g = cb_ref[:, cs] + cw[2:3, :] * gate
        g = g + cw[1:2, :] * gbuf[SUBLANES - 1:SUBLANES - 1 + tm, :]
        g = g + cw[0:1, :] * gbuf[SUBLANES - 2:SUBLANES - 2 + tm, :]
        act = (g * jax.nn.sigmoid(g) * up).astype(BF16)
        acc_ref[...] += _dot(act, wd_ref[cs, :])
    y_ref[...] = _rmsnorm(acc_ref[...], gfin_ref[...])


def _ffn(x2, hf, wg, wu, cw, cb, wd, gfin, seq):
    T = x2.shape[0]
    tm = TOKEN_TILE
    full = lambda shape: pl.BlockSpec(shape, lambda i: (0,) * len(shape), pipeline_mode=pl.Buffered(1))
    tok = pl.BlockSpec((tm, D_MODEL), lambda i: (i, 0))
    return pl.pallas_call(
        functools.partial(_ffn_kernel, seq // tm),
        grid=(T // tm,),
        in_specs=[tok, tok, full((D_MODEL, D_FF)), full((D_MODEL, D_FF)), full((FFN_CONV_WIDTH, D_FF)),
                  full((1, D_FF)), full((D_FF, D_MODEL)), full((1, D_MODEL))],
        out_specs=tok,
        out_shape=jax.ShapeDtypeStruct((T, D_MODEL), F32),
        scratch_shapes=[
            pltpu.VMEM((SUBLANES, D_FF), F32),
            pltpu.VMEM((SUBLANES + tm, FF_CHUNK), F32),
            pltpu.VMEM((tm, D_MODEL), F32),
        ],
        compiler_params=pltpu.CompilerParams(
            dimension_semantics=("arbitrary",), vmem_limit_bytes=VMEM_LIMIT_BYTES),
        name="ffn",
    )(x2, hf, wg, wu, cw, cb, wd, gfin)


def kernel(x, mem, positions, norm_mix_g, w_in, w_out, conv_b_w, conv_b_b, ln_b_g, ln_b_b, norm_cross_g, norm_mem_g, w_q_cross, w_k_cross, w_v_cross, w_o_cross, norm_ffn_g, w_gate, w_up, ffn_conv_w, ffn_conv_b, w_down, norm_final_g):
    batch, seq, _ = x.shape
    n_mem = mem.shape[1]
    depth = w_in.shape[0]
    assert depth == 1 and seq % TOKEN_TILE == 0 and seq % QUERY_TILE == 0

    row = lambda v: v.reshape(1, -1)
    x2d = x.reshape(batch * seq, D_MODEL)
    pos = positions.reshape(1, batch * seq)
    freq = (ROPE_THETA ** (-jnp.arange(0, ROT_DIM, 2, dtype=F32) / ROT_DIM)).reshape(ROT_HALF, 1)

    wi_ = w_in[0]
    c_q, c_k, c_v, c_qi, c_ki, c_wi = 0, A_WIDTH, A_WIDTH + 64, A_WIDTH + 128, A_WIDTH + 384, A_WIDTH + 448
    c_glu = c_wi + IDX_HEADS
    wt = jnp.concatenate([
        wi_[:, c_q:c_k], wi_[:, c_qi:c_ki], wi_[:, c_k:c_v], wi_[:, c_ki:c_wi], wi_[:, c_v:c_qi],
        wi_[:, c_wi:c_glu], jnp.zeros((D_MODEL, ROWS_T - ROW_WI - IDX_HEADS), F32)], axis=1).T.astype(BF16)
    wg = wi_[:, c_glu:].astype(BF16)

    qT, qiT, k, ki, vT, wiT, b_out = _in_proj(
        pos, x2d, row(norm_mix_g[0]), wt, wg, freq, conv_b_w[0], row(conv_b_b[0]), row(ln_b_g[0]),
        row(ln_b_b[0]), seq)
    a_out = _dsa(qT, qiT, wiT, k, ki, vT, batch, seq)

    kc, vc = _mem_kv(mem.reshape(batch * n_mem, D_MODEL), row(norm_mem_g[0]),
                     w_k_cross[0].astype(BF16), w_v_cross[0].astype(BF16))
    wo_mix = w_out[0].astype(BF16)
    x2, hf = _mid(x2d, a_out, b_out, wo_mix[:A_WIDTH], wo_mix[A_WIDTH:], row(norm_cross_g[0]),
                  w_q_cross[0].astype(BF16), kc, vc, w_o_cross[0].astype(BF16), row(norm_ffn_g[0]), seq, n_mem)
    y = _ffn(x2, hf, w_gate[0].astype(BF16), w_up[0].astype(BF16), ffn_conv_w[0], row(ffn_conv_b[0]),
             w_down[0].astype(BF16), row(norm_final_g), seq)
    return y.reshape(batch, seq, D_MODEL)
```

```python
import functools

import jax
import jax.numpy as jnp
from jax import lax
from jax.experimental import pallas as pl
from jax.experimental.pallas import tpu as pltpu

F32 = jnp.float32
BF16 = jnp.bfloat16

D_MODEL = 1024
A_HEADS = 8
HEAD_DIM = 64
A_WIDTH = A_HEADS * HEAD_DIM
IDX_HEADS = 4
IDX_DIM = 64
TOPK_MAX = 256
B_WIDTH = D_MODEL - A_WIDTH
CONV_B_WIDTH = 31
ROPE_THETA = 500000.0
ROT_DIM = HEAD_DIM // 4
ROT_HALF = ROT_DIM // 2
CROSS_HEADS = 4
CROSS_HEAD_DIM = D_MODEL // CROSS_HEADS
D_FF = 2816
FFN_CONV_WIDTH = 3
EPS = 1e-6

SUBLANES = 8
TOKEN_TILE = 512
QUERY_TILE = 512
CONV_HALO = 32
CONV_ROWS = 32
SCORE_ROWS = 512
FF_CHUNK = 256
VMEM_LIMIT_BYTES = 56 * 1024 * 1024

ROW_Q = 0
ROW_QI = ROW_Q + A_WIDTH
ROW_K = ROW_QI + IDX_HEADS * IDX_DIM
ROW_KI = ROW_K + HEAD_DIM
ROW_V = ROW_KI + IDX_DIM
ROW_WI = ROW_V + HEAD_DIM
ROWS_T = ROW_WI + 16

V_ROWS = HEAD_DIM + 16
LOG2_E = 1.4426950408889634

INT_MIN = -(2 ** 31)
KEY_NEG_INF = INT_MIN + 0x7FFFFF
NEG_INF = float("-inf")


def _rmsnorm(x, g):
    ms = jnp.mean(x * x, axis=-1, keepdims=True)
    return (x * lax.rsqrt(ms + EPS)) * g


def _dot(a, b):
    return jnp.dot(a, b, preferred_element_type=F32)


def _dot_nt(a, b):
    return lax.dot_general(a, b, (((1,), (1,)), ((), ())), preferred_element_type=F32)


def _in_proj_kernel(tiles_per_seq, pos_ref, x_ref, g_ref, wt_ref, wg_ref, freq_ref, cw_ref, cb_ref,
                    lng_ref, lnb_ref, qT_ref, qiT_ref, k_ref, ki_ref, vT_ref, wiT_ref, b_ref, ubuf):
    tm = x_ref.shape[0]
    h = _rmsnorm(x_ref[...], g_ref[...]).astype(BF16)

    r = _dot_nt(wt_ref[...], h)
    ang = freq_ref[...] * pos_ref[...].astype(F32)
    cos = jnp.cos(ang)
    sin = jnp.sin(ang)

    def rope(xt, heads):
        x3 = xt.reshape(heads, HEAD_DIM, tm)
        x1 = x3[:, 0:ROT_HALF, :]
        x2 = x3[:, ROT_HALF:ROT_DIM, :]
        out = jnp.concatenate([x1 * cos - x2 * sin, x2 * cos + x1 * sin, x3[:, ROT_DIM:, :]], axis=1)
        return out.reshape(heads * HEAD_DIM, tm)

    qT_ref[...] = (rope(r[ROW_Q:ROW_QI], A_HEADS) * (HEAD_DIM ** -0.5 * LOG2_E)).astype(BF16)
    qiT_ref[...] = (rope(r[ROW_QI:ROW_K], IDX_HEADS) * (IDX_DIM ** -0.5)).astype(BF16)
    kk = rope(r[ROW_K:ROW_V], 2).T
    k_ref[...] = kk[:, :HEAD_DIM].astype(BF16)
    ki_ref[...] = kk[:, HEAD_DIM:].astype(BF16)
    ones_rows = (lax.broadcasted_iota(jnp.int32, (V_ROWS - HEAD_DIM, tm), 0) == 0).astype(F32)
    vT_ref[...] = jnp.concatenate([r[ROW_V:ROW_WI], ones_rows], axis=0).astype(BF16)
    wiT_ref[...] = r[ROW_WI:ROW_WI + SUBLANES] * (IDX_HEADS ** -0.5)

    glu = _dot(h, wg_ref[...])
    u = glu[:, :B_WIDTH] * jax.nn.sigmoid(glu[:, B_WIDTH:])

    @pl.when(pl.program_id(0) % tiles_per_seq == 0)
    def _():
        ubuf[0:CONV_HALO, :] = jnp.zeros((CONV_HALO, B_WIDTH), F32)

    ubuf[CONV_HALO:CONV_HALO + tm, :] = u
    base = CONV_HALO - (CONV_B_WIDTH - 1)
    for c in range(tm // CONV_ROWS):
        acc = jnp.broadcast_to(cb_ref[...], (CONV_ROWS, B_WIDTH))
        for j in range(CONV_B_WIDTH):
            acc = acc + ubuf[c * CONV_ROWS + base + j:c * CONV_ROWS + base + j + CONV_ROWS, :] * cw_ref[j:j + 1, :]
        mu = jnp.mean(acc, axis=-1, keepdims=True)
        var = jnp.mean(jnp.square(acc - mu), axis=-1, keepdims=True)
        y = (acc - mu) * lax.rsqrt(var + EPS) * lng_ref[...] + lnb_ref[...]
        b_ref[c * CONV_ROWS:(c + 1) * CONV_ROWS, :] = (y * jax.nn.sigmoid(y)).astype(BF16)
    ubuf[0:CONV_HALO, :] = ubuf[tm:tm + CONV_HALO, :]


def _in_proj(pos, x2d, g, wt, wg, freq, cw, cb, lng, lnb, seq):
    T = x2d.shape[0]
    tm = TOKEN_TILE
    full = lambda shape: pl.BlockSpec(shape, lambda i: (0,) * len(shape))
    return pl.pallas_call(
        functools.partial(_in_proj_kernel, seq // tm),
        grid=(T // tm,),
        in_specs=[
            pl.BlockSpec((1, tm), lambda i: (0, i)),
            pl.BlockSpec((tm, D_MODEL), lambda i: (i, 0)),
            full((1, D_MODEL)),
            full((ROWS_T, D_MODEL)),
            full((D_MODEL, 2 * B_WIDTH)),
            full((ROT_HALF, 1)),
            full((CONV_B_WIDTH, B_WIDTH)),
            full((1, B_WIDTH)),
            full((1, B_WIDTH)),
            full((1, B_WIDTH)),
        ],
        out_specs=[
            pl.BlockSpec((A_WIDTH, tm), lambda i: (0, i)),
            pl.BlockSpec((IDX_HEADS * IDX_DIM, tm), lambda i: (0, i)),
            pl.BlockSpec((tm, HEAD_DIM), lambda i: (i, 0)),
            pl.BlockSpec((tm, IDX_DIM), lambda i: (i, 0)),
            pl.BlockSpec((V_ROWS, tm), lambda i: (0, i)),
            pl.BlockSpec((SUBLANES, tm), lambda i: (0, i)),
            pl.BlockSpec((tm, B_WIDTH), lambda i: (i, 0)),
        ],
        out_shape=[
            jax.ShapeDtypeStruct((A_WIDTH, T), BF16),
            jax.ShapeDtypeStruct((IDX_HEADS * IDX_DIM, T), BF16),
            jax.ShapeDtypeStruct((T, HEAD_DIM), BF16),
            jax.ShapeDtypeStruct((T, IDX_DIM), BF16),
            jax.ShapeDtypeStruct((V_ROWS, T), BF16),
            jax.ShapeDtypeStruct((SUBLANES, T), F32),
            jax.ShapeDtypeStruct((T, B_WIDTH), BF16),
        ],
        scratch_shapes=[pltpu.VMEM((CONV_HALO + tm, B_WIDTH), F32)],
        compiler_params=pltpu.CompilerParams(
            dimension_semantics=("arbitrary",), vmem_limit_bytes=VMEM_LIMIT_BYTES),
        name="in_proj",
    )(pos, x2d, g, wt, wg, freq, cw, cb, lng, lnb)


def _key_to_f32(key):
    bits = jnp.where(key >= 0, key, key ^ jnp.int32(0x7FFFFFFF))
    return lax.bitcast_convert_type(bits, F32)


def _dsa_kernel(qT_ref, qiT_ref, wT_ref, k_ref, ki_ref, vT_ref, o_ref, sc_ref, lg_ref, acc_ref, m_ref, cm_ref):
    tq = qT_ref.shape[1]
    kc = tq
    qi_blk = pl.program_id(1)
    nchunk = qi_blk + 1

    row = lax.broadcasted_iota(jnp.int32, (kc, tq), 0)
    col = lax.broadcasted_iota(jnp.int32, (kc, tq), 1)
    causal = row <= col

    def chunk_start(c):
        return pl.multiple_of(c * kc, kc)

    w = wT_ref[...]

    def score_body(c, carry):
        r0 = pl.multiple_of(c * SCORE_ROWS, SCORE_ROWS)
        ki_c = ki_ref[pl.ds(r0, SCORE_ROWS), :]
        s = jnp.zeros((SCORE_ROWS, tq), F32)
        for h in range(IDX_HEADS):
            d = _dot(ki_c, qiT_ref[h * IDX_DIM:(h + 1) * IDX_DIM, :])
            s = s + w[h:h + 1, :] * jnp.maximum(d, 0.0)
        sc_ref[pl.ds(r0, SCORE_ROWS), :] = s
        return carry

    lax.fori_loop(0, nchunk * (kc // SCORE_ROWS), score_body, 0)
    d0 = chunk_start(qi_blk)
    sc_ref[pl.ds(d0, kc), :] = jnp.where(causal, sc_ref[pl.ds(d0, kc), :], NEG_INF)

    def count_ge(thr):
        def body(c, acc):
            blk = sc_ref[pl.ds(chunk_start(c), kc), :]
            ge = (blk >= thr).astype(jnp.int32)
            return acc + ge.reshape(kc // 32, 32, tq).sum(axis=0)
        acc = lax.fori_loop(0, nchunk, body, jnp.zeros((32, tq), jnp.int32))
        return acc.sum(axis=0, keepdims=True)

    def bit_body(it, t):
        trial = t + lax.shift_left(jnp.int32(1), 31 - it)
        cnt = count_ge(_key_to_f32(trial))
        return jnp.where(cnt >= TOPK_MAX, trial, t)

    t = lax.fori_loop(0, 32, bit_body, jnp.full((1, tq), INT_MIN, jnp.int32))
    tau = _key_to_f32(jnp.maximum(t, KEY_NEG_INF))

    def gt_body(c, acc):
        blk = sc_ref[pl.ds(chunk_start(c), kc), :]
        return acc + (blk > tau).astype(jnp.int32).reshape(kc // 32, 32, tq).sum(axis=0)

    n_gt = lax.fori_loop(0, nchunk, gt_body, jnp.zeros((32, tq), jnp.int32)).sum(axis=0, keepdims=True)
    need = (TOPK_MAX - n_gt).astype(F32)

    lower_tri = (row > col).astype(BF16)

    def bias_body(c, carry):
        r0 = chunk_start(c)
        blk = sc_ref[pl.ds(r0, kc), :]
        eq = blk == tau
        eqf = jnp.where(eq, 1.0, 0.0)
        rank = _dot(lower_tri, eqf.astype(BF16)) + carry
        sel = (blk > tau) | (eq & (rank < need))
        sc_ref[pl.ds(r0, kc), :] = jnp.where(sel, 0.0, NEG_INF)
        return carry + eqf.sum(axis=0, keepdims=True)

    lax.fori_loop(0, nchunk, bias_body, jnp.zeros((1, tq), F32))
    sc_ref[pl.ds(d0, kc), :] = jnp.where(causal, sc_ref[pl.ds(d0, kc), :], NEG_INF)

    m_ref[...] = jnp.full(m_ref.shape, NEG_INF, F32)
    cm_ref[...] = jnp.full(cm_ref.shape, NEG_INF, F32)
    acc_ref[...] = jnp.zeros(acc_ref.shape, F32)
    lg_ref[...] = jnp.full(lg_ref.shape, NEG_INF, F32)

    def att_body(c, carry):
        r_prev = chunk_start(jnp.maximum(c - 1, 0))
        vT_c = vT_ref[:, pl.ds(r_prev, kc)]
        m_old = m_ref[...]
        m_new = jnp.maximum(m_old, cm_ref[...])
        m_safe = jnp.where(m_new == NEG_INF, 0.0, m_new)
        alpha = jnp.exp2(m_old - m_safe)
        m_ref[...] = m_new
        for h in range(A_HEADS):
            p = jnp.exp2(lg_ref[h] - m_safe[h:h + 1, :]).astype(BF16)
            acc_ref[h] = alpha[h:h + 1, :] * acc_ref[h] + _dot(vT_c, p)

        r0 = chunk_start(jnp.minimum(c, nchunk - 1))
        k_c = k_ref[pl.ds(r0, kc), :]
        bias = sc_ref[pl.ds(r0, kc), :]
        cmax = []
        for h in range(A_HEADS):
            logit = _dot(k_c, qT_ref[h * HEAD_DIM:(h + 1) * HEAD_DIM, :]) + bias
            lg_ref[h] = logit
            cmax.append(logit.max(axis=0, keepdims=True))
        cm_ref[...] = jnp.concatenate(cmax, axis=0)
        return carry

    lax.fori_loop(0, nchunk + 1, att_body, 0)

    outs = []
    for h in range(A_HEADS):
        a = acc_ref[h]
        outs.append(a[:HEAD_DIM, :] * (1.0 / a[HEAD_DIM:HEAD_DIM + 1, :]))
    o_ref[...] = jnp.concatenate(outs, axis=0).T.astype(BF16)


def _dsa(qT, qiT, wiT, k, ki, vT, batch, seq):
    tq = QUERY_TILE
    nq = seq // tq
    T = batch * seq
    return pl.pallas_call(
        _dsa_kernel,
        grid=(batch, nq),
        in_specs=[
            pl.BlockSpec((A_WIDTH, tq), lambda b, i: (0, b * nq + i)),
            pl.BlockSpec((IDX_HEADS * IDX_DIM, tq), lambda b, i: (0, b * nq + i)),
            pl.BlockSpec((SUBLANES, tq), lambda b, i: (0, b * nq + i)),
            pl.BlockSpec((seq, HEAD_DIM), lambda b, i: (b, 0)),
            pl.BlockSpec((seq, IDX_DIM), lambda b, i: (b, 0)),
            pl.BlockSpec((V_ROWS, seq), lambda b, i: (0, b)),
        ],
        out_specs=pl.BlockSpec((tq, A_WIDTH), lambda b, i: (b * nq + i, 0)),
        out_shape=jax.ShapeDtypeStruct((T, A_WIDTH), BF16),
        scratch_shapes=[
            pltpu.VMEM((seq, tq), F32),
            pltpu.VMEM((A_HEADS, tq, tq), F32),
            pltpu.VMEM((A_HEADS, V_ROWS, tq), F32),
            pltpu.VMEM((A_HEADS, tq), F32),
            pltpu.VMEM((A_HEADS, tq), F32),
        ],
        compiler_params=pltpu.CompilerParams(
            dimension_semantics=("arbitrary", "arbitrary"), vmem_limit_bytes=VMEM_LIMIT_BYTES),
        name="dsa",
    )(qT, qiT, wiT, k, ki, vT)


def _mem_kv_kernel(mem_ref, g_ref, wk_ref, wv_ref, kc_ref, vc_ref):
    m = _rmsnorm(mem_ref[...], g_ref[...]).astype(BF16)
    kc_ref[...] = _dot(m, wk_ref[...]).astype(BF16)
    vc_ref[...] = _dot(m, wv_ref[...]).astype(BF16)


def _mem_kv(mem2d, g, wk, wv):
    M = mem2d.shape[0]
    tm = TOKEN_TILE
    full = lambda shape: pl.BlockSpec(shape, lambda i: (0,) * len(shape))
    return pl.pallas_call(
        _mem_kv_kernel,
        grid=(M // tm,),
        in_specs=[pl.BlockSpec((tm, D_MODEL), lambda i: (i, 0)), full((1, D_MODEL)),
                  full((D_MODEL, D_MODEL)), full((D_MODEL, D_MODEL))],
        out_specs=[pl.BlockSpec((tm, D_MODEL), lambda i: (i, 0))] * 2,
        out_shape=[jax.ShapeDtypeStruct((M, D_MODEL), BF16)] * 2,
        compiler_params=pltpu.CompilerParams(
            dimension_semantics=("arbitrary",), vmem_limit_bytes=VMEM_LIMIT_BYTES),
        name="mem_kv",
    )(mem2d, g, wk, wv)


def _mid_kernel(x_ref, a_ref, b_ref, wa_ref, wb_ref, gc_ref, wq_ref, kc_ref, vc_ref, wo_ref, gf_ref,
                x2_ref, hf_ref):
    x1 = x_ref[...] + _dot(a_ref[...], wa_ref[...]) + _dot(b_ref[...], wb_ref[...])
    hq = _rmsnorm(x1, gc_ref[...]).astype(BF16)
    qc = (_dot(hq, wq_ref[...]) * (CROSS_HEAD_DIM ** -0.5)).astype(BF16)
    heads = []
    for h in range(CROSS_HEADS):
        hs = slice(h * CROSS_HEAD_DIM, (h + 1) * CROSS_HEAD_DIM)
        logit = _dot_nt(qc[:, hs], kc_ref[:, hs])
        p = jnp.exp(logit - logit.max(axis=-1, keepdims=True))
        p = p * (1.0 / p.sum(axis=-1, keepdims=True))
        heads.append(_dot(p.astype(BF16), vc_ref[:, hs]))
    oc = jnp.concatenate(heads, axis=-1).astype(BF16)
    x2 = x1 + _dot(oc, wo_ref[...])
    x2_ref[...] = x2
    hf_ref[...] = _rmsnorm(x2, gf_ref[...]).astype(BF16)


def _mid(x2d, a, b, wa, wb, gc, wq, kc, vc, wo, gf, seq, n_mem):
    T = x2d.shape[0]
    tm = TOKEN_TILE
    tiles_per_seq = seq // tm
    full = lambda shape: pl.BlockSpec(shape, lambda i: (0,) * len(shape))
    tok = lambda width: pl.BlockSpec((tm, width), lambda i: (i, 0))
    mem = pl.BlockSpec((n_mem, D_MODEL), lambda i: (i // tiles_per_seq, 0))
    return pl.pallas_call(
        _mid_kernel,
        grid=(T // tm,),
        in_specs=[tok(D_MODEL), tok(A_WIDTH), tok(B_WIDTH), full((A_WIDTH, D_MODEL)), full((B_WIDTH, D_MODEL)),
                  full((1, D_MODEL)), full((D_MODEL, D_MODEL)), mem, mem, full((D_MODEL, D_MODEL)),
                  full((1, D_MODEL))],
        out_specs=[tok(D_MODEL), tok(D_MODEL)],
        out_shape=[jax.ShapeDtypeStruct((T, D_MODEL), F32), jax.ShapeDtypeStruct((T, D_MODEL), BF16)],
        compiler_params=pltpu.CompilerParams(
            dimension_semantics=("arbitrary",), vmem_limit_bytes=VMEM_LIMIT_BYTES),
        name="mid",
    )(x2d, a, b, wa, wb, gc, wq, kc, vc, wo, gf)


def _ffn_kernel(tiles_per_seq, x_ref, h_ref, wg_ref, wu_ref, cw_ref, cb_ref, wd_ref, gfin_ref, y_ref,
                halo, gbuf, acc_ref):
    tm = x_ref.shape[0]

    @pl.when(pl.program_id(0) % tiles_per_seq == 0)
    def _():
        halo[...] = jnp.zeros(halo.shape, F32)

    h = h_ref[...]
    acc_ref[...] = x_ref[...]
    for c in range(D_FF // FF_CHUNK):
        cs = slice(c * FF_CHUNK, (c + 1) * FF_CHUNK)
        gate = _dot(h, wg_ref[:, cs])
        up = _dot(h, wu_ref[:, cs])
        gbuf[0:SUBLANES, :] = halo[:, cs]
        gbuf[SUBLANES:SUBLANES + tm, :] = gate
        halo[:, cs] = gate[tm - SUBLANES:, :]
        cw = cw_ref[:, cs]
        g = cb_ref[:, cs] + cw[2:3, :] * gate
        g = g + cw[1:2, :] * gbuf[SUBLANES - 1:SUBLANES - 1 + tm, :]
        g = g + cw[0:1, :] * gbuf[SUBLANES - 2:SUBLANES - 2 + tm, :]
        act = (g * jax.nn.sigmoid(g) * up).astype(BF16)
        acc_ref[...] += _dot(act, wd_ref[cs, :])
    y_ref[...] = _rmsnorm(acc_ref[...], gfin_ref[...])


def _ffn(x2, hf, wg, wu, cw, cb, wd, gfin, seq):
    T = x2.shape[0]
    tm = TOKEN_TILE
    full = lambda shape: pl.BlockSpec(shape, lambda i: (0,) * len(shape), pipeline_mode=pl.Buffered(1))
    tok = pl.BlockSpec((tm, D_MODEL), lambda i: (i, 0))
    return pl.pallas_call(
        functools.partial(_ffn_kernel, seq // tm),
        grid=(T // tm,),
        in_specs=[tok, tok, full((D_MODEL, D_FF)), full((D_MODEL, D_FF)), full((FFN_CONV_WIDTH, D_FF)),
                  full((1, D_FF)), full((D_FF, D_MODEL)), full((1, D_MODEL))],
        out_specs=tok,
        out_shape=jax.ShapeDtypeStruct((T, D_MODEL), F32),
        scratch_shapes=[
            pltpu.VMEM((SUBLANES, D_FF), F32),
            pltpu.VMEM((SUBLANES + tm, FF_CHUNK), F32),
            pltpu.VMEM((tm, D_MODEL), F32),
        ],
        compiler_params=pltpu.CompilerParams(
            dimension_semantics=("arbitrary",), vmem_limit_bytes=VMEM_LIMIT_BYTES),
        name="ffn",
    )(x2, hf, wg, wu, cw, cb, wd, gfin)


def kernel(x, mem, positions, norm_mix_g, w_in, w_out, conv_b_w, conv_b_b, ln_b_g, ln_b_b, norm_cross_g, norm_mem_g, w_q_cross, w_k_cross, w_v_cross, w_o_cross, norm_ffn_g, w_gate, w_up, ffn_conv_w, ffn_conv_b, w_down, norm_final_g):
    batch, seq, _ = x.shape
    n_mem = mem.shape[1]
    depth = w_in.shape[0]
    assert depth == 1 and seq % TOKEN_TILE == 0 and seq % QUERY_TILE == 0

    row = lambda v: v.reshape(1, -1)
    x2d = x.reshape(batch * seq, D_MODEL)
    pos = positions.reshape(1, batch * seq)
    freq = (ROPE_THETA ** (-jnp.arange(0, ROT_DIM, 2, dtype=F32) / ROT_DIM)).reshape(ROT_HALF, 1)

    wi_ = w_in[0]
    c_q, c_k, c_v, c_qi, c_ki, c_wi = 0, A_WIDTH, A_WIDTH + 64, A_WIDTH + 128, A_WIDTH + 384, A_WIDTH + 448
    c_glu = c_wi + IDX_HEADS
    wt = jnp.concatenate([
        wi_[:, c_q:c_k], wi_[:, c_qi:c_ki], wi_[:, c_k:c_v], wi_[:, c_ki:c_wi], wi_[:, c_v:c_qi],
        wi_[:, c_wi:c_glu], jnp.zeros((D_MODEL, ROWS_T - ROW_WI - IDX_HEADS), F32)], axis=1).T.astype(BF16)
    wg = wi_[:, c_glu:].astype(BF16)

    qT, qiT, k, ki, vT, wiT, b_out = _in_proj(
        pos, x2d, row(norm_mix_g[0]), wt, wg, freq, conv_b_w[0], row(conv_b_b[0]), row(ln_b_g[0]),
        row(ln_b_b[0]), seq)
    a_out = _dsa(qT, qiT, wiT, k, ki, vT, batch, seq)

    kc, vc = _mem_kv(mem.reshape(batch * n_mem, D_MODEL), row(norm_mem_g[0]),
                     w_k_cross[0].astype(BF16), w_v_cross[0].astype(BF16))
    wo_mix = w_out[0].astype(BF16)
    x2, hf = _mid(x2d, a_out, b_out, wo_mix[:A_WIDTH], wo_mix[A_WIDTH:], row(norm_cross_g[0]),
                  w_q_cross[0].astype(BF16), kc, vc, w_o_cross[0].astype(BF16), row(norm_ffn_g[0]), seq, n_mem)
    y = _ffn(x2, hf, w_gate[0].astype(BF16), w_up[0].astype(BF16), ffn_conv_w[0], row(ffn_conv_b[0]),
             w_down[0].astype(BF16), row(norm_final_g), seq)
    return y.reshape(batch, seq, D_MODEL)
```

```python
import functools

import jax
import jax.numpy as jnp
from jax import lax
from jax.experimental import pallas as pl
from jax.experimental.pallas import tpu as pltpu

F32 = jnp.float32
BF16 = jnp.bfloat16

D_MODEL = 1024
A_HEADS = 8
HEAD_DIM = 64
A_WIDTH = A_HEADS * HEAD_DIM
IDX_HEADS = 4
IDX_DIM = 64
TOPK_MAX = 256
B_WIDTH = D_MODEL - A_WIDTH
CONV_B_WIDTH = 31
ROPE_THETA = 500000.0
ROT_DIM = HEAD_DIM // 4
ROT_HALF = ROT_DIM // 2
CROSS_HEADS = 4
CROSS_HEAD_DIM = D_MODEL // CROSS_HEADS
D_FF = 2816
FFN_CONV_WIDTH = 3
EPS = 1e-6

SUBLANES = 8
TOKEN_TILE = 512
QUERY_TILE = 512
CONV_HALO = 32
CONV_ROWS = 64
SCORE_ROWS = 512
COUNT_ROWS = 32
FF_CHUNK = 256
VMEM_LIMIT_BYTES = 56 * 1024 * 1024

ROW_Q = 0
ROW_QI = ROW_Q + A_WIDTH
ROW_K = ROW_QI + IDX_HEADS * IDX_DIM
ROW_KI = ROW_K + HEAD_DIM
ROW_V = ROW_KI + IDX_DIM
ROW_WI = ROW_V + HEAD_DIM
ROWS_T = ROW_WI + 16

V_ROWS = HEAD_DIM + 16
LOG2_E = 1.4426950408889634

INT_MIN = -(2 ** 31)
KEY_NEG_INF = INT_MIN + 0x7FFFFF
NEG_INF = float("-inf")


def _rmsnorm(x, g):
    ms = jnp.mean(x * x, axis=-1, keepdims=True)
    return (x * lax.rsqrt(ms + EPS)) * g


def _dot(a, b):
    return jnp.dot(a, b, preferred_element_type=F32)


def _dot_nt(a, b):
    return lax.dot_general(a, b, (((1,), (1,)), ((), ())), preferred_element_type=F32)


def _in_proj_kernel(tiles_per_seq, pos_ref, x_ref, g_ref, wt_ref, wg_ref, freq_ref, cw_ref, cb_ref,
                    lng_ref, lnb_ref, qT_ref, qiT_ref, k_ref, ki_ref, vT_ref, wiT_ref, b_ref, ubuf):
    tm = x_ref.shape[0]
    h = _rmsnorm(x_ref[...], g_ref[...]).astype(BF16)

    r = _dot_nt(wt_ref[...], h)
    ang = freq_ref[...] * pos_ref[...].astype(F32)
    cos = jnp.cos(ang)
    sin = jnp.sin(ang)

    def rope(xt, heads):
        x3 = xt.reshape(heads, HEAD_DIM, tm)
        x1 = x3[:, 0:ROT_HALF, :]
        x2 = x3[:, ROT_HALF:ROT_DIM, :]
        out = jnp.concatenate([x1 * cos - x2 * sin, x2 * cos + x1 * sin, x3[:, ROT_DIM:, :]], axis=1)
        return out.reshape(heads * HEAD_DIM, tm)

    qT_ref[...] = (rope(r[ROW_Q:ROW_QI], A_HEADS) * (HEAD_DIM ** -0.5 * LOG2_E)).astype(BF16)
    qiT_ref[...] = (rope(r[ROW_QI:ROW_K], IDX_HEADS) * (IDX_DIM ** -0.5)).astype(BF16)
    kk = rope(r[ROW_K:ROW_V], 2).T
    k_ref[...] = kk[:, :HEAD_DIM].astype(BF16)
    ki_ref[...] = kk[:, HEAD_DIM:].astype(BF16)
    ones_rows = (lax.broadcasted_iota(jnp.int32, (V_ROWS - HEAD_DIM, tm), 0) == 0).astype(F32)
    vT_ref[...] = jnp.concatenate([r[ROW_V:ROW_WI], ones_rows], axis=0).astype(BF16)
    wiT_ref[...] = r[ROW_WI:ROW_WI + SUBLANES] * (IDX_HEADS ** -0.5)

    glu = _dot(h, wg_ref[...])
    u = glu[:, :B_WIDTH] * jax.nn.sigmoid(glu[:, B_WIDTH:])

    @pl.when(pl.program_id(0) % tiles_per_seq == 0)
    def _():
        ubuf[0:CONV_HALO, :] = jnp.zeros((CONV_HALO, B_WIDTH), F32)

    ubuf[CONV_HALO:CONV_HALO + tm, :] = u
    base = CONV_HALO - (CONV_B_WIDTH - 1)
    for c in range(tm // CONV_ROWS):
        c0 = c * CONV_ROWS
        halves = []
        for ls in (slice(0, B_WIDTH // 2), slice(B_WIDTH // 2, B_WIDTH)):
            acc = jnp.broadcast_to(cb_ref[:, ls], (CONV_ROWS, B_WIDTH // 2))
            for r in range(SUBLANES):
                rows = CONV_ROWS + (SUBLANES if r else 0)
                q = None
                for j in range(CONV_B_WIDTH):
                    if (base + j) % SUBLANES != r:
                        continue
                    a0 = c0 + (base + j) // SUBLANES * SUBLANES
                    term = ubuf[a0:a0 + rows, ls] * cw_ref[j:j + 1, ls]
                    q = term if q is None else q + term
                acc = acc + q[r:r + CONV_ROWS, :]
            halves.append(acc)
        acc = jnp.concatenate(halves, axis=1)
        mu = jnp.mean(acc, axis=-1, keepdims=True)
        var = jnp.mean(jnp.square(acc - mu), axis=-1, keepdims=True)
        y = (acc - mu) * lax.rsqrt(var + EPS) * lng_ref[...] + lnb_ref[...]
        b_ref[c * CONV_ROWS:(c + 1) * CONV_ROWS, :] = (y * jax.nn.sigmoid(y)).astype(BF16)
    ubuf[0:CONV_HALO, :] = ubuf[tm:tm + CONV_HALO, :]


def _in_proj(pos, x2d, g, wt, wg, freq, cw, cb, lng, lnb, seq):
    T = x2d.shape[0]
    tm = TOKEN_TILE
    full = lambda shape: pl.BlockSpec(shape, lambda i: (0,) * len(shape))
    return pl.pallas_call(
        functools.partial(_in_proj_kernel, seq // tm),
        grid=(T // tm,),
        in_specs=[
            pl.BlockSpec((1, tm), lambda i: (0, i)),
            pl.BlockSpec((tm, D_MODEL), lambda i: (i, 0)),
            full((1, D_MODEL)),
            full((ROWS_T, D_MODEL)),
            full((D_MODEL, 2 * B_WIDTH)),
            full((ROT_HALF, 1)),
            full((CONV_B_WIDTH, B_WIDTH)),
            full((1, B_WIDTH)),
            full((1, B_WIDTH)),
            full((1, B_WIDTH)),
        ],
        out_specs=[
            pl.BlockSpec((A_WIDTH, tm), lambda i: (0, i)),
            pl.BlockSpec((IDX_HEADS * IDX_DIM, tm), lambda i: (0, i)),
            pl.BlockSpec((tm, HEAD_DIM), lambda i: (i, 0)),
            pl.BlockSpec((tm, IDX_DIM), lambda i: (i, 0)),
            pl.BlockSpec((V_ROWS, tm), lambda i: (0, i)),
            pl.BlockSpec((SUBLANES, tm), lambda i: (0, i)),
            pl.BlockSpec((tm, B_WIDTH), lambda i: (i, 0)),
        ],
        out_shape=[
            jax.ShapeDtypeStruct((A_WIDTH, T), BF16),
            jax.ShapeDtypeStruct((IDX_HEADS * IDX_DIM, T), BF16),
            jax.ShapeDtypeStruct((T, HEAD_DIM), BF16),
            jax.ShapeDtypeStruct((T, IDX_DIM), BF16),
            jax.ShapeDtypeStruct((V_ROWS, T), BF16),
            jax.ShapeDtypeStruct((SUBLANES, T), F32),
            jax.ShapeDtypeStruct((T, B_WIDTH), BF16),
        ],
        scratch_shapes=[pltpu.VMEM((CONV_HALO + tm, B_WIDTH), F32)],
        compiler_params=pltpu.CompilerParams(
            dimension_semantics=("arbitrary",), vmem_limit_bytes=VMEM_LIMIT_BYTES),
        name="in_proj",
    )(pos, x2d, g, wt, wg, freq, cw, cb, lng, lnb)


def _key_to_f32(key):
    bits = jnp.where(key >= 0, key, key ^ jnp.int32(0x7FFFFFFF))
    return lax.bitcast_convert_type(bits, F32)


def _high_half(x):
    bits = lax.bitcast_convert_type(x, jnp.int32) & jnp.int32(-(2 ** 16))
    return lax.bitcast_convert_type(bits, F32).astype(BF16)


def _dsa_kernel(qT_ref, qiT_ref, wT_ref, k_ref, ki_ref, vT_ref, o_ref, sc_ref, sh_ref, lg_ref, acc_ref, m_ref,
                cm_ref):
    tq = qT_ref.shape[1]
    kc = tq
    qi_blk = pl.program_id(1)
    nchunk = qi_blk + 1

    row = lax.broadcasted_iota(jnp.int32, (kc, tq), 0)
    col = lax.broadcasted_iota(jnp.int32, (kc, tq), 1)
    causal = row <= col

    def chunk_start(c):
        return pl.multiple_of(c * kc, kc)

    w = wT_ref[...]

    def score_body(c, carry):
        r0 = pl.multiple_of(c * SCORE_ROWS, SCORE_ROWS)
        ki_c = ki_ref[pl.ds(r0, SCORE_ROWS), :]
        s = jnp.zeros((SCORE_ROWS, tq), F32)
        for h in range(IDX_HEADS):
            d = _dot(ki_c, qiT_ref[h * IDX_DIM:(h + 1) * IDX_DIM, :])
            s = s + w[h:h + 1, :] * jnp.maximum(d, 0.0)
        sc_ref[pl.ds(r0, SCORE_ROWS), :] = s
        sh_ref[pl.ds(r0, SCORE_ROWS), :] = _high_half(s)
        return carry

    lax.fori_loop(0, nchunk * (kc // SCORE_ROWS), score_body, 0)
    d0 = chunk_start(qi_blk)
    diag = jnp.where(causal, sc_ref[pl.ds(d0, kc), :], NEG_INF)
    sc_ref[pl.ds(d0, kc), :] = diag
    sh_ref[pl.ds(d0, kc), :] = _high_half(diag)

    def count_ge16(thr):
        one = jnp.ones((), BF16)
        zero = jnp.zeros((), BF16)

        def body(c, acc):
            r0 = chunk_start(c)
            for g in range(kc // COUNT_ROWS):
                blk = sh_ref[pl.ds(r0 + g * COUNT_ROWS, COUNT_ROWS), :]
                acc = acc + jnp.where(blk >= thr, one, zero)
            return acc
        acc = lax.fori_loop(0, nchunk, body, jnp.zeros((COUNT_ROWS, tq), BF16))
        return acc.astype(F32).sum(axis=0, keepdims=True)

    def bit16_body(it, t):
        trial = t + lax.shift_left(jnp.int32(1), 15 - it)
        bits = jnp.where(trial >= 0, trial, trial ^ jnp.int32(0x7FFF))
        thr = lax.bitcast_convert_type(lax.shift_left(bits, 16), F32).astype(BF16)
        return jnp.where(count_ge16(thr) >= TOPK_MAX, trial, t)

    t16 = lax.fori_loop(0, 16, bit16_body, jnp.full((1, tq), -(2 ** 15), jnp.int32))

    def count_ge(thr):
        def body(c, acc):
            blk = sc_ref[pl.ds(chunk_start(c), kc), :]
            ge = (blk >= thr).astype(jnp.int32)
            return acc + ge.reshape(kc // COUNT_ROWS, COUNT_ROWS, tq).sum(axis=0)
        acc = lax.fori_loop(0, nchunk, body, jnp.zeros((COUNT_ROWS, tq), jnp.int32))
        return acc.sum(axis=0, keepdims=True)

    def bit_body(it, t):
        trial = t + lax.shift_left(jnp.int32(1), 15 - it)
        cnt = count_ge(_key_to_f32(trial))
        return jnp.where(cnt >= TOPK_MAX, trial, t)

    t = lax.fori_loop(0, 16, bit_body, lax.shift_left(t16, 16))
    tau = _key_to_f32(jnp.maximum(t, KEY_NEG_INF))

    def gt_body(c, acc):
        blk = sc_ref[pl.ds(chunk_start(c), kc), :]
        return acc + (blk > tau).astype(jnp.int32).reshape(kc // COUNT_ROWS, COUNT_ROWS, tq).sum(axis=0)

    n_gt = lax.fori_loop(0, nchunk, gt_body, jnp.zeros((COUNT_ROWS, tq), jnp.int32)).sum(axis=0, keepdims=True)
    need = (TOPK_MAX - n_gt).astype(F32)

    lower_tri = (row > col).astype(BF16)

    def bias_body(c, carry):
        r0 = chunk_start(c)
        blk = sc_ref[pl.ds(r0, kc), :]
        eq = blk == tau
        eqf = jnp.where(eq, 1.0, 0.0)
        rank = _dot(lower_tri, eqf.astype(BF16)) + carry
        sel = (blk > tau) | (eq & (rank < need))
        sc_ref[pl.ds(r0, kc), :] = jnp.where(sel, 0.0, NEG_INF)
        return carry + eqf.sum(axis=0, keepdims=True)

    lax.fori_loop(0, nchunk, bias_body, jnp.zeros((1, tq), F32))
    sc_ref[pl.ds(d0, kc), :] = jnp.where(causal, sc_ref[pl.ds(d0, kc), :], NEG_INF)

    m_ref[...] = jnp.full(m_ref.shape, NEG_INF, F32)
    cm_ref[...] = jnp.full(cm_ref.shape, NEG_INF, F32)
    acc_ref[...] = jnp.zeros(acc_ref.shape, F32)
    lg_ref[...] = jnp.full(lg_ref.shape, NEG_INF, F32)

    def att_body(c, carry):
        r_prev = chunk_start(jnp.maximum(c - 1, 0))
        vT_c = vT_ref[:, pl.ds(r_prev, kc)]
        m_old = m_ref[...]
        m_new = jnp.maximum(m_old, cm_ref[...])
        m_safe = jnp.where(m_new == NEG_INF, 0.0, m_new)
        alpha = jnp.exp2(m_old - m_safe)
        m_ref[...] = m_new
        for h in range(A_HEADS):
            p = jnp.exp2(lg_ref[h] - m_safe[h:h + 1, :]).astype(BF16)
            acc_ref[h] = alpha[h:h + 1, :] * acc_ref[h] + _dot(vT_c, p)

        r0 = chunk_start(jnp.minimum(c, nchunk - 1))
        k_c = k_ref[pl.ds(r0, kc), :]
        bias = sc_ref[pl.ds(r0, kc), :]
        cmax = []
        for h in range(A_HEADS):
            logit = _dot(k_c, qT_ref[h * HEAD_DIM:(h + 1) * HEAD_DIM, :]) + bias
            lg_ref[h] = logit
            cmax.append(logit.max(axis=0, keepdims=True))
        cm_ref[...] = jnp.concatenate(cmax, axis=0)
        return carry

    lax.fori_loop(0, nchunk + 1, att_body, 0)

    outs = []
    for h in range(A_HEADS):
        a = acc_ref[h]
        outs.append(a[:HEAD_DIM, :] * (1.0 / a[HEAD_DIM:HEAD_DIM + 1, :]))
    o_ref[...] = jnp.concatenate(outs, axis=0).T.astype(BF16)


def _dsa(qT, qiT, wiT, k, ki, vT, batch, seq):
    tq = QUERY_TILE
    nq = seq // tq
    T = batch * seq
    return pl.pallas_call(
        _dsa_kernel,
        grid=(batch, nq),
        in_specs=[
            pl.BlockSpec((A_WIDTH, tq), lambda b, i: (0, b * nq + i)),
            pl.BlockSpec((IDX_HEADS * IDX_DIM, tq), lambda b, i: (0, b * nq + i)),
            pl.BlockSpec((SUBLANES, tq), lambda b, i: (0, b * nq + i)),
            pl.BlockSpec((seq, HEAD_DIM), lambda b, i: (b, 0)),
            pl.BlockSpec((seq, IDX_DIM), lambda b, i: (b, 0)),
            pl.BlockSpec((V_ROWS, seq), lambda b, i: (0, b)),
        ],
        out_specs=pl.BlockSpec((tq, A_WIDTH), lambda b, i: (b * nq + i, 0)),
        out_shape=jax.ShapeDtypeStruct((T, A_WIDTH), BF16),
        scratch_shapes=[
            pltpu.VMEM((seq, tq), F32),
            pltpu.VMEM((seq, tq), BF16),
            pltpu.VMEM((A_HEADS, tq, tq), F32),
            pltpu.VMEM((A_HEADS, V_ROWS, tq), F32),
            pltpu.VMEM((A_HEADS, tq), F32),
            pltpu.VMEM((A_HEADS, tq), F32),
        ],
        compiler_params=pltpu.CompilerParams(
            dimension_semantics=("arbitrary", "arbitrary"), vmem_limit_bytes=VMEM_LIMIT_BYTES),
        name="dsa",
    )(qT, qiT, wiT, k, ki, vT)


def _mem_kv_kernel(mem_ref, g_ref, wk_ref, wv_ref, kc_ref, vc_ref):
    m = _rmsnorm(mem_ref[...], g_ref[...]).astype(BF16)
    kc_ref[...] = _dot(m, wk_ref[...]).astype(BF16)
    vc_ref[...] = _dot(m, wv_ref[...]).astype(BF16)


def _mem_kv(mem2d, g, wk, wv):
    M = mem2d.shape[0]
    tm = TOKEN_TILE
    full = lambda shape: pl.BlockSpec(shape, lambda i: (0,) * len(shape))
    return pl.pallas_call(
        _mem_kv_kernel,
        grid=(M // tm,),
        in_specs=[pl.BlockSpec((tm, D_MODEL), lambda i: (i, 0)), full((1, D_MODEL)),
                  full((D_MODEL, D_MODEL)), full((D_MODEL, D_MODEL))],
        out_specs=[pl.BlockSpec((tm, D_MODEL), lambda i: (i, 0))] * 2,
        out_shape=[jax.ShapeDtypeStruct((M, D_MODEL), BF16)] * 2,
        compiler_params=pltpu.CompilerParams(
            dimension_semantics=("arbitrary",), vmem_limit_bytes=VMEM_LIMIT_BYTES),
        name="mem_kv",
    )(mem2d, g, wk, wv)


def _mid_kernel(x_ref, a_ref, b_ref, wa_ref, wb_ref, gc_ref, wq_ref, kc_ref, vc_ref, wo_ref, gf_ref,
                x2_ref, hf_ref):
    x1 = x_ref[...] + _dot(a_ref[...], wa_ref[...]) + _dot(b_ref[...], wb_ref[...])
    hq = _rmsnorm(x1, gc_ref[...]).astype(BF16)
    qc = (_dot(hq, wq_ref[...]) * (CROSS_HEAD_DIM ** -0.5)).astype(BF16)
    heads = []
    for h in range(CROSS_HEADS):
        hs = slice(h * CROSS_HEAD_DIM, (h + 1) * CROSS_HEAD_DIM)
        logit = _dot_nt(qc[:, hs], kc_ref[:, hs])
        p = jnp.exp(logit - logit.max(axis=-1, keepdims=True))
        p = p * (1.0 / p.sum(axis=-1, keepdims=True))
        heads.append(_dot(p.astype(BF16), vc_ref[:, hs]))
    oc = jnp.concatenate(heads, axis=-1).astype(BF16)
    x2 = x1 + _dot(oc, wo_ref[...])
    x2_ref[...] = x2
    hf_ref[...] = _rmsnorm(x2, gf_ref[...]).astype(BF16)


def _mid(x2d, a, b, wa, wb, gc, wq, kc, vc, wo, gf, seq, n_mem):
    T = x2d.shape[0]
    tm = TOKEN_TILE
    tiles_per_seq = seq // tm
    full = lambda shape: pl.BlockSpec(shape, lambda i: (0,) * len(shape))
    tok = lambda width: pl.BlockSpec((tm, width), lambda i: (i, 0))
    mem = pl.BlockSpec((n_mem, D_MODEL), lambda i: (i // tiles_per_seq, 0))
    return pl.pallas_call(
        _mid_kernel,
        grid=(T // tm,),
        in_specs=[tok(D_MODEL), tok(A_WIDTH), tok(B_WIDTH), full((A_WIDTH, D_MODEL)), full((B_WIDTH, D_MODEL)),
                  full((1, D_MODEL)), full((D_MODEL, D_MODEL)), mem, mem, full((D_MODEL, D_MODEL)),
                  full((1, D_MODEL))],
        out_specs=[tok(D_MODEL), tok(D_MODEL)],
        out_shape=[jax.ShapeDtypeStruct((T, D_MODEL), F32), jax.ShapeDtypeStruct((T, D_MODEL), BF16)],
        compiler_params=pltpu.CompilerParams(
            dimension_semantics=("arbitrary",), vmem_limit_bytes=VMEM_LIMIT_BYTES),
        name="mid",
    )(x2d, a, b, wa, wb, gc, wq, kc, vc, wo, gf)


def _ffn_kernel(tiles_per_seq, x_ref, h_ref, wg_ref, wu_ref, cw_ref, cb_ref, wd_ref, gfin_ref, y_ref,
                halo, gbuf, acc_ref):
    tm = x_ref.shape[0]

    @pl.when(pl.program_id(0) % tiles_per_seq == 0)
    def _():
        halo[...] = jnp.zeros(halo.shape, F32)

    h = h_ref[...]
    acc_ref[...] = x_ref[...]
    for c in range(D_FF // FF_CHUNK):
        cs = slice(c * FF_CHUNK, (c + 1) * FF_CHUNK)
        gate = _dot(h, wg_ref[:, cs])
        up = _dot(h, wu_ref[:, cs])
        gbuf[0:SUBLANES, :] = halo[:, cs]
        gbuf[SUBLANES:SUBLANES + tm, :] = gate
        halo[:, cs] = gate[tm - SUBLANES:, :]
        cw = cw_ref[:, cs]
        g = cb_ref[:, cs] + cw[2:3, :] * gate
        g = g + cw[1:2, :] * gbuf[SUBLANES - 1:SUBLANES - 1 + tm, :]
        g = g + cw[0:1, :] * gbuf[SUBLANES - 2:SUBLANES - 2 + tm, :]
        act = (g * jax.nn.sigmoid(g) * up).astype(BF16)
        acc_ref[...] += _dot(act, wd_ref[cs, :])
    y_ref[...] = _rmsnorm(acc_ref[...], gfin_ref[...])


def _ffn(x2, hf, wg, wu, cw, cb, wd, gfin, seq):
    T = x2.shape[0]
    tm = TOKEN_TILE
    full = lambda shape: pl.BlockSpec(shape, lambda i: (0,) * len(shape), pipeline_mode=pl.Buffered(1))
    tok = pl.BlockSpec((tm, D_MODEL), lambda i: (i, 0))
    return pl.pallas_call(
        functools.partial(_ffn_kernel, seq // tm),
        grid=(T // tm,),
        in_specs=[tok, tok, full((D_MODEL, D_FF)), full((D_MODEL, D_FF)), full((FFN_CONV_WIDTH, D_FF)),
                  full((1, D_FF)), full((D_FF, D_MODEL)), full((1, D_MODEL))],
        out_specs=tok,
        out_shape=jax.ShapeDtypeStruct((T, D_MODEL), F32),
        scratch_shapes=[
            pltpu.VMEM((SUBLANES, D_FF), F32),
            pltpu.VMEM((SUBLANES + tm, FF_CHUNK), F32),
            pltpu.VMEM((tm, D_MODEL), F32),
        ],
        compiler_params=pltpu.CompilerParams(
            dimension_semantics=("arbitrary",), vmem_limit_bytes=VMEM_LIMIT_BYTES),
        name="ffn",
    )(x2, hf, wg, wu, cw, cb, wd, gfin)


def kernel(x, mem, positions, norm_mix_g, w_in, w_out, conv_b_w, conv_b_b, ln_b_g, ln_b_b, norm_cross_g, norm_mem_g, w_q_cross, w_k_cross, w_v_cross, w_o_cross, norm_ffn_g, w_gate, w_up, ffn_conv_w, ffn_conv_b, w_down, norm_final_g):
    batch, seq, _ = x.shape
    n_mem = mem.shape[1]
    depth = w_in.shape[0]
    assert depth == 1 and seq % TOKEN_TILE == 0 and seq % QUERY_TILE == 0

    row = lambda v: v.reshape(1, -1)
    x2d = x.reshape(batch * seq, D_MODEL)
    pos = positions.reshape(1, batch * seq)
    freq = (ROPE_THETA ** (-jnp.arange(0, ROT_DIM, 2, dtype=F32) / ROT_DIM)).reshape(ROT_HALF, 1)

    wi_ = w_in[0]
    c_q, c_k, c_v, c_qi, c_ki, c_wi = 0, A_WIDTH, A_WIDTH + 64, A_WIDTH + 128, A_WIDTH + 384, A_WIDTH + 448
    c_glu = c_wi + IDX_HEADS
    wt = jnp.concatenate([
        wi_[:, c_q:c_k], wi_[:, c_qi:c_ki], wi_[:, c_k:c_v], wi_[:, c_ki:c_wi], wi_[:, c_v:c_qi],
        wi_[:, c_wi:c_glu], jnp.zeros((D_MODEL, ROWS_T - ROW_WI - IDX_HEADS), F32)], axis=1).T.astype(BF16)
    wg = wi_[:, c_glu:].astype(BF16)

    qT, qiT, k, ki, vT, wiT, b_out = _in_proj(
        pos, x2d, row(norm_mix_g[0]), wt, wg, freq, conv_b_w[0], row(conv_b_b[0]), row(ln_b_g[0]),
        row(ln_b_b[0]), seq)
    a_out = _dsa(qT, qiT, wiT, k, ki, vT, batch, seq)

    kc, vc = _mem_kv(mem.reshape(batch * n_mem, D_MODEL), row(norm_mem_g[0]),
                     w_k_cross[0].astype(BF16), w_v_cross[0].astype(BF16))
    wo_mix = w_out[0].astype(BF16)
    x2, hf = _mid(x2d, a_out, b_out, wo_mix[:A_WIDTH], wo_mix[A_WIDTH:], row(norm_cross_g[0]),
                  w_q_cross[0].astype(BF16), kc, vc, w_o_cross[0].astype(BF16), row(norm_ffn_g[0]), seq, n_mem)
    y = _ffn(x2, hf, w_gate[0].astype(BF16), w_up[0].astype(BF16), ffn_conv_w[0], row(ffn_conv_b[0]),
             w_down[0].astype(BF16), row(norm_final_g), seq)
    return y.reshape(batch, seq, D_MODEL)
```

```python
import functools

import jax
import jax.numpy as jnp
from jax import lax
from jax.experimental import pallas as pl
from jax.experimental.pallas import tpu as pltpu

F32 = jnp.float32
BF16 = jnp.bfloat16

D_MODEL = 1024
A_HEADS = 8
HEAD_DIM = 64
A_WIDTH = A_HEADS * HEAD_DIM
IDX_HEADS = 4
IDX_DIM = 64
TOPK_MAX = 256
B_WIDTH = D_MODEL - A_WIDTH
CONV_B_WIDTH = 31
ROPE_THETA = 500000.0
ROT_DIM = HEAD_DIM // 4
ROT_HALF = ROT_DIM // 2
CROSS_HEADS = 4
CROSS_HEAD_DIM = D_MODEL // CROSS_HEADS
D_FF = 2816
FFN_CONV_WIDTH = 3
EPS = 1e-6

SUBLANES = 8
TOKEN_TILE = 512
FFN_TILE = 1024
QUERY_TILE = 512
CONV_HALO = 32
CONV_ROWS = 64
SCORE_ROWS = 512
COUNT_ROWS = 32
FF_CHUNK = 256
VMEM_LIMIT_BYTES = 56 * 1024 * 1024

ROW_Q = 0
ROW_QI = ROW_Q + A_WIDTH
ROW_K = ROW_QI + IDX_HEADS * IDX_DIM
ROW_KI = ROW_K + HEAD_DIM
ROW_V = ROW_KI + IDX_DIM
ROW_WI = ROW_V + HEAD_DIM
ROWS_T = ROW_WI + 16

V_ROWS = HEAD_DIM + 16
LOG2_E = 1.4426950408889634

INT_MIN = -(2 ** 31)
KEY_NEG_INF = INT_MIN + 0x7FFFFF
NEG_INF = float("-inf")


def _rmsnorm(x, g):
    ms = jnp.mean(x * x, axis=-1, keepdims=True)
    return (x * lax.rsqrt(ms + EPS)) * g


def _dot(a, b):
    return jnp.dot(a, b, preferred_element_type=F32)


def _dot_nt(a, b):
    return lax.dot_general(a, b, (((1,), (1,)), ((), ())), preferred_element_type=F32)


def _in_proj_kernel(pos_ref, x_ref, g_ref, wt_ref, freq_ref, qT_ref, qiT_ref, k_ref, ki_ref, vT_ref, wiT_ref):
    tm = x_ref.shape[0]
    h = _rmsnorm(x_ref[...], g_ref[...]).astype(BF16)

    r = _dot_nt(wt_ref[...], h)
    ang = freq_ref[...] * pos_ref[...].astype(F32)
    cos = jnp.cos(ang)
    sin = jnp.sin(ang)

    def rope(xt, heads):
        x3 = xt.reshape(heads, HEAD_DIM, tm)
        x1 = x3[:, 0:ROT_HALF, :]
        x2 = x3[:, ROT_HALF:ROT_DIM, :]
        out = jnp.concatenate([x1 * cos - x2 * sin, x2 * cos + x1 * sin, x3[:, ROT_DIM:, :]], axis=1)
        return out.reshape(heads * HEAD_DIM, tm)

    qT_ref[...] = (rope(r[ROW_Q:ROW_QI], A_HEADS) * (HEAD_DIM ** -0.5 * LOG2_E)).astype(BF16)
    qiT_ref[...] = (rope(r[ROW_QI:ROW_K], IDX_HEADS) * (IDX_DIM ** -0.5)).astype(BF16)
    kk = rope(r[ROW_K:ROW_V], 2).T
    k_ref[...] = kk[:, :HEAD_DIM].astype(BF16)
    ki_ref[...] = kk[:, HEAD_DIM:].astype(BF16)
    ones_rows = (lax.broadcasted_iota(jnp.int32, (V_ROWS - HEAD_DIM, tm), 0) == 0).astype(F32)
    vT_ref[...] = jnp.concatenate([r[ROW_V:ROW_WI], ones_rows], axis=0).astype(BF16)
    wiT_ref[...] = r[ROW_WI:ROW_WI + SUBLANES] * (IDX_HEADS ** -0.5)


def _group_b_steps(x, g_ref, wg_ref, cw_ref, cb_ref, lng_ref, lnb_ref, ubuf, b_buf):
    tm = x.shape[0]
    h = _rmsnorm(x, g_ref[...]).astype(BF16)
    glu = _dot(h, wg_ref[...])
    u = glu[:, :B_WIDTH] * jax.nn.sigmoid(glu[:, B_WIDTH:])
    ubuf[CONV_HALO:CONV_HALO + tm, :] = u
    base = CONV_HALO - (CONV_B_WIDTH - 1)
    n_chunks = tm // CONV_ROWS

    def chunk(c):
        c0 = c * CONV_ROWS
        halves = []
        for ls in (slice(0, B_WIDTH // 2), slice(B_WIDTH // 2, B_WIDTH)):
            acc = jnp.broadcast_to(cb_ref[:, ls], (CONV_ROWS, B_WIDTH // 2))
            for r in range(SUBLANES):
                rows = CONV_ROWS + (SUBLANES if r else 0)
                q = None
                for j in range(CONV_B_WIDTH):
                    if (base + j) % SUBLANES != r:
                        continue
                    a0 = c0 + (base + j) // SUBLANES * SUBLANES
                    term = ubuf[a0:a0 + rows, ls] * cw_ref[j:j + 1, ls]
                    q = term if q is None else q + term
                acc = acc + q[r:r + CONV_ROWS, :]
            halves.append(acc)
        acc = jnp.concatenate(halves, axis=1)
        mu = jnp.mean(acc, axis=-1, keepdims=True)
        var = jnp.mean(jnp.square(acc - mu), axis=-1, keepdims=True)
        y = (acc - mu) * lax.rsqrt(var + EPS) * lng_ref[...] + lnb_ref[...]
        b_buf[c * CONV_ROWS:(c + 1) * CONV_ROWS, :] = (y * jax.nn.sigmoid(y)).astype(BF16)
        if c == n_chunks - 1:
            ubuf[0:CONV_HALO, :] = ubuf[tm:tm + CONV_HALO, :]

    return [functools.partial(chunk, c) for c in range(n_chunks)]


def _in_proj(pos, x2d, g, wt, freq):
    T = x2d.shape[0]
    tm = TOKEN_TILE
    full = lambda shape: pl.BlockSpec(shape, lambda i: (0,) * len(shape))
    return pl.pallas_call(
        _in_proj_kernel,
        grid=(T // tm,),
        in_specs=[
            pl.BlockSpec((1, tm), lambda i: (0, i)),
            pl.BlockSpec((tm, D_MODEL), lambda i: (i, 0)),
            full((1, D_MODEL)),
            full((ROWS_T, D_MODEL)),
            full((ROT_HALF, 1)),
        ],
        out_specs=[
            pl.BlockSpec((A_WIDTH, tm), lambda i: (0, i)),
            pl.BlockSpec((IDX_HEADS * IDX_DIM, tm), lambda i: (0, i)),
            pl.BlockSpec((tm, HEAD_DIM), lambda i: (i, 0)),
            pl.BlockSpec((tm, IDX_DIM), lambda i: (i, 0)),
            pl.BlockSpec((V_ROWS, tm), lambda i: (0, i)),
            pl.BlockSpec((SUBLANES, tm), lambda i: (0, i)),
        ],
        out_shape=[
            jax.ShapeDtypeStruct((A_WIDTH, T), BF16),
            jax.ShapeDtypeStruct((IDX_HEADS * IDX_DIM, T), BF16),
            jax.ShapeDtypeStruct((T, HEAD_DIM), BF16),
            jax.ShapeDtypeStruct((T, IDX_DIM), BF16),
            jax.ShapeDtypeStruct((V_ROWS, T), BF16),
            jax.ShapeDtypeStruct((SUBLANES, T), F32),
        ],
        compiler_params=pltpu.CompilerParams(
            dimension_semantics=("arbitrary",), vmem_limit_bytes=VMEM_LIMIT_BYTES),
        name="in_proj",
    )(pos, x2d, g, wt, freq)


def _key_to_f32(key):
    bits = jnp.where(key >= 0, key, key ^ jnp.int32(0x7FFFFFFF))
    return lax.bitcast_convert_type(bits, F32)


def _high_half(x):
    bits = lax.bitcast_convert_type(x, jnp.int32) & jnp.int32(-(2 ** 16))
    return lax.bitcast_convert_type(bits, F32).astype(BF16)


def _dsa_kernel(qT_ref, qiT_ref, wT_ref, k_ref, ki_ref, vT_ref, o_ref, sc_ref, sh_ref, lg_ref, acc_ref, m_ref,
                cm_ref):
    tq = qT_ref.shape[1]
    kc = tq
    qi_blk = pl.program_id(1)
    nchunk = qi_blk + 1

    row = lax.broadcasted_iota(jnp.int32, (kc, tq), 0)
    col = lax.broadcasted_iota(jnp.int32, (kc, tq), 1)
    causal = row <= col

    def chunk_start(c):
        return pl.multiple_of(c * kc, kc)

    w = wT_ref[...]

    def score_body(c, carry):
        r0 = pl.multiple_of(c * SCORE_ROWS, SCORE_ROWS)
        ki_c = ki_ref[pl.ds(r0, SCORE_ROWS), :]
        s = jnp.zeros((SCORE_ROWS, tq), F32)
        for h in range(IDX_HEADS):
            d = _dot(ki_c, qiT_ref[h * IDX_DIM:(h + 1) * IDX_DIM, :])
            s = s + w[h:h + 1, :] * jnp.maximum(d, 0.0)
        sc_ref[pl.ds(r0, SCORE_ROWS), :] = s
        sh_ref[pl.ds(r0, SCORE_ROWS), :] = _high_half(s)
        return carry

    lax.fori_loop(0, nchunk * (kc // SCORE_ROWS), score_body, 0)
    d0 = chunk_start(qi_blk)
    diag = jnp.where(causal, sc_ref[pl.ds(d0, kc), :], NEG_INF)
    sc_ref[pl.ds(d0, kc), :] = diag
    sh_ref[pl.ds(d0, kc), :] = _high_half(diag)

    def count_ge16(thr):
        one = jnp.ones((), BF16)
        zero = jnp.zeros((), BF16)

        def body(c, acc):
            r0 = chunk_start(c)
            for g in range(kc // COUNT_ROWS):
                blk = sh_ref[pl.ds(r0 + g * COUNT_ROWS, COUNT_ROWS), :]
                acc = acc + jnp.where(blk >= thr, one, zero)
            return acc
        acc = lax.fori_loop(0, nchunk, body, jnp.zeros((COUNT_ROWS, tq), BF16))
        return acc.astype(F32).sum(axis=0, keepdims=True)

    def bit16_body(it, t):
        trial = t + lax.shift_left(jnp.int32(1), 15 - it)
        bits = jnp.where(trial >= 0, trial, trial ^ jnp.int32(0x7FFF))
        thr = lax.bitcast_convert_type(lax.shift_left(bits, 16), F32).astype(BF16)
        return jnp.where(count_ge16(thr) >= TOPK_MAX, trial, t)

    t16 = lax.fori_loop(0, 16, bit16_body, jnp.full((1, tq), -(2 ** 15), jnp.int32))

    def count_ge(thr):
        def body(c, acc):
            blk = sc_ref[pl.ds(chunk_start(c), kc), :]
            ge = (blk >= thr).astype(jnp.int32)
            return acc + ge.reshape(kc // COUNT_ROWS, COUNT_ROWS, tq).sum(axis=0)
        acc = lax.fori_loop(0, nchunk, body, jnp.zeros((COUNT_ROWS, tq), jnp.int32))
        return acc.sum(axis=0, keepdims=True)

    def bit_body(it, t):
        trial = t + lax.shift_left(jnp.int32(1), 15 - it)
        cnt = count_ge(_key_to_f32(trial))
        return jnp.where(cnt >= TOPK_MAX, trial, t)

    t = lax.fori_loop(0, 16, bit_body, lax.shift_left(t16, 16))
    tau = _key_to_f32(jnp.maximum(t, KEY_NEG_INF))

    def gt_body(c, acc):
        blk = sc_ref[pl.ds(chunk_start(c), kc), :]
        return acc + (blk > tau).astype(jnp.int32).reshape(kc // COUNT_ROWS, COUNT_ROWS, tq).sum(axis=0)

    n_gt = lax.fori_loop(0, nchunk, gt_body, jnp.zeros((COUNT_ROWS, tq), jnp.int32)).sum(axis=0, keepdims=True)
    need = (TOPK_MAX - n_gt).astype(F32)

    lower_tri = (row > col).astype(BF16)

    def bias_body(c, carry):
        r0 = chunk_start(c)
        counts = []
        for ls in (slice(0, tq // 2), slice(tq // 2, tq)):
            blk = sc_ref[pl.ds(r0, kc), ls]
            eq = blk == tau[:, ls]
            eqf = jnp.where(eq, 1.0, 0.0)
            rank = _dot(lower_tri, eqf.astype(BF16)) + carry[:, ls]
            sel = (blk > tau[:, ls]) | (eq & (rank < need[:, ls]))
            sc_ref[pl.ds(r0, kc), ls] = jnp.where(sel, 0.0, NEG_INF)
            counts.append(eqf.sum(axis=0, keepdims=True))
        return carry + jnp.concatenate(counts, axis=1)

    lax.fori_loop(0, nchunk, bias_body, jnp.zeros((1, tq), F32))
    sc_ref[pl.ds(d0, kc), :] = jnp.where(causal, sc_ref[pl.ds(d0, kc), :], NEG_INF)

    m_ref[...] = jnp.full(m_ref.shape, NEG_INF, F32)
    cm_ref[...] = jnp.full(cm_ref.shape, NEG_INF, F32)
    acc_ref[...] = jnp.zeros(acc_ref.shape, F32)
    lg_ref[...] = jnp.full(lg_ref.shape, NEG_INF, F32)

    def att_body(c, carry):
        r_prev = chunk_start(jnp.maximum(c - 1, 0))
        vT_c = vT_ref[:, pl.ds(r_prev, kc)]
        m_old = m_ref[...]
        m_new = jnp.maximum(m_old, cm_ref[...])
        m_safe = jnp.where(m_new == NEG_INF, 0.0, m_new)
        alpha = jnp.exp2(m_old - m_safe)
        m_ref[...] = m_new
        r0 = chunk_start(jnp.minimum(c, nchunk - 1))
        k_c = k_ref[pl.ds(r0, kc), :]
        bias = sc_ref[pl.ds(r0, kc), :]
        cmax = []
        for h in range(A_HEADS):
            hmax = []
            for ls in (slice(0, tq // 2), slice(tq // 2, tq)):
                p = jnp.exp2(lg_ref[h, :, ls] - m_safe[h:h + 1, ls]).astype(BF16)
                acc_ref[h, :, ls] = alpha[h:h + 1, ls] * acc_ref[h, :, ls] + _dot(vT_c, p)
                logit = _dot(k_c, qT_ref[h * HEAD_DIM:(h + 1) * HEAD_DIM, ls]) + bias[:, ls]
                lg_ref[h, :, ls] = logit
                hmax.append(logit.max(axis=0, keepdims=True))
            cmax.append(jnp.concatenate(hmax, axis=1))
        cm_ref[...] = jnp.concatenate(cmax, axis=0)
        return carry

    lax.fori_loop(0, nchunk + 1, att_body, 0)

    outs = []
    for h in range(A_HEADS):
        a = acc_ref[h]
        outs.append(a[:HEAD_DIM, :] * (1.0 / a[HEAD_DIM:HEAD_DIM + 1, :]))
    o_ref[...] = jnp.concatenate(outs, axis=0).T.astype(BF16)


def _dsa(qT, qiT, wiT, k, ki, vT, batch, seq):
    tq = QUERY_TILE
    nq = seq // tq
    T = batch * seq
    return pl.pallas_call(
        _dsa_kernel,
        grid=(batch, nq),
        in_specs=[
            pl.BlockSpec((A_WIDTH, tq), lambda b, i: (0, b * nq + i)),
            pl.BlockSpec((IDX_HEADS * IDX_DIM, tq), lambda b, i: (0, b * nq + i)),
            pl.BlockSpec((SUBLANES, tq), lambda b, i: (0, b * nq + i)),
            pl.BlockSpec((seq, HEAD_DIM), lambda b, i: (b, 0)),
            pl.BlockSpec((seq, IDX_DIM), lambda b, i: (b, 0)),
            pl.BlockSpec((V_ROWS, seq), lambda b, i: (0, b)),
        ],
        out_specs=pl.BlockSpec((tq, A_WIDTH), lambda b, i: (b * nq + i, 0)),
        out_shape=jax.ShapeDtypeStruct((T, A_WIDTH), BF16),
        scratch_shapes=[
            pltpu.VMEM((seq, tq), F32),
            pltpu.VMEM((seq, tq), BF16),
            pltpu.VMEM((A_HEADS, tq, tq), F32),
            pltpu.VMEM((A_HEADS, V_ROWS, tq), F32),
            pltpu.VMEM((A_HEADS, tq), F32),
            pltpu.VMEM((A_HEADS, tq), F32),
        ],
        compiler_params=pltpu.CompilerParams(
            dimension_semantics=("arbitrary", "arbitrary"), vmem_limit_bytes=VMEM_LIMIT_BYTES),
        name="dsa",
    )(qT, qiT, wiT, k, ki, vT)


def _mem_kv_kernel(mem_ref, g_ref, wk_ref, wv_ref, kc_ref, vc_ref):
    m = _rmsnorm(mem_ref[...], g_ref[...]).astype(BF16)
    kc_ref[...] = _dot(m, wk_ref[...]).astype(BF16)
    vc_ref[...] = _dot(m, wv_ref[...]).astype(BF16)


def _mem_kv(mem2d, g, wk, wv):
    M = mem2d.shape[0]
    tm = TOKEN_TILE
    full = lambda shape: pl.BlockSpec(shape, lambda i: (0,) * len(shape))
    return pl.pallas_call(
        _mem_kv_kernel,
        grid=(M // tm,),
        in_specs=[pl.BlockSpec((tm, D_MODEL), lambda i: (i, 0)), full((1, D_MODEL)),
                  full((D_MODEL, D_MODEL)), full((D_MODEL, D_MODEL))],
        out_specs=[pl.BlockSpec((tm, D_MODEL), lambda i: (i, 0))] * 2,
        out_shape=[jax.ShapeDtypeStruct((M, D_MODEL), BF16)] * 2,
        compiler_params=pltpu.CompilerParams(
            dimension_semantics=("arbitrary",), vmem_limit_bytes=VMEM_LIMIT_BYTES),
        name="mem_kv",
    )(mem2d, g, wk, wv)


def _mid_kernel(tiles_per_seq, xn_ref, x_ref, a_ref, gm_ref, wg_ref, cw_ref, cb_ref, lng_ref, lnb_ref, wa_ref,
                wb_ref, gc_ref, wq_ref, kc_ref, vc_ref, wo_ref, gf_ref, x2_ref, hf_ref, ubuf, b_buf):
    step = pl.program_id(0)

    @pl.when(step == 0)
    def _():
        b_buf[...] = jnp.zeros(b_buf.shape, BF16)

    @pl.when(step % tiles_per_seq == 0)
    def _():
        ubuf[0:CONV_HALO, :] = jnp.zeros((CONV_HALO, B_WIDTH), F32)

    x = x_ref[...]
    a = a_ref[...]
    b = b_buf[...]
    conv_chunks = _group_b_steps(xn_ref[...], gm_ref, wg_ref, cw_ref, cb_ref, lng_ref, lnb_ref, ubuf, b_buf)
    stages = 4 * CROSS_HEADS
    per_stage = -(-len(conv_chunks) // stages)

    def run_conv_chunks():
        for _ in range(min(per_stage, len(conv_chunks))):
            conv_chunks.pop(0)()

    head_cols = [slice(h * CROSS_HEAD_DIM, (h + 1) * CROSS_HEAD_DIM) for h in range(CROSS_HEADS)]
    x1_cols = []
    for cs in head_cols:
        x1_cols.append(x[:, cs] + _dot(a, wa_ref[:, cs]) + _dot(b, wb_ref[:, cs]))
        run_conv_chunks()
    x1 = jnp.concatenate(x1_cols, axis=-1)
    hq = _rmsnorm(x1, gc_ref[...]).astype(BF16)
    heads = []
    for hs in head_cols:
        qc = (_dot(hq, wq_ref[:, hs]) * (CROSS_HEAD_DIM ** -0.5)).astype(BF16)
        run_conv_chunks()
        logit = _dot_nt(qc, kc_ref[:, hs])
        p = jnp.exp(logit - logit.max(axis=-1, keepdims=True))
        p = p * (1.0 / p.sum(axis=-1, keepdims=True))
        heads.append(_dot(p.astype(BF16), vc_ref[:, hs]))
        run_conv_chunks()
    oc = jnp.concatenate(heads, axis=-1).astype(BF16)
    x2_cols = []
    for cs in head_cols:
        x2_cols.append(x1[:, cs] + _dot(oc, wo_ref[:, cs]))
        run_conv_chunks()
    while conv_chunks:
        run_conv_chunks()
    x2 = jnp.concatenate(x2_cols, axis=-1)
    x2_ref[...] = x2
    hf_ref[...] = _rmsnorm(x2, gf_ref[...]).astype(BF16)


def _mid(x2d, a, gm, wg, cw, cb, lng, lnb, wa, wb, gc, wq, kc, vc, wo, gf, seq, n_mem):
    T = x2d.shape[0]
    tm = FFN_TILE
    n_tiles = T // tm
    tiles_per_seq = seq // tm
    full = lambda shape: pl.BlockSpec(shape, lambda i: (0,) * len(shape), pipeline_mode=pl.Buffered(1))
    prev = lambda i: jnp.maximum(i - 1, 0)
    tok = lambda width: pl.BlockSpec((tm, width), lambda i: (prev(i), 0))
    mem = pl.BlockSpec((n_mem, D_MODEL), lambda i: (prev(i) // tiles_per_seq, 0))
    return pl.pallas_call(
        functools.partial(_mid_kernel, tiles_per_seq),
        grid=(n_tiles + 1,),
        in_specs=[pl.BlockSpec((tm, D_MODEL), lambda i: (jnp.minimum(i, n_tiles - 1), 0)),
                  tok(D_MODEL), tok(A_WIDTH), full((1, D_MODEL)), full((D_MODEL, 2 * B_WIDTH)),
                  full((CONV_B_WIDTH, B_WIDTH)), full((1, B_WIDTH)), full((1, B_WIDTH)), full((1, B_WIDTH)),
                  full((A_WIDTH, D_MODEL)), full((B_WIDTH, D_MODEL)), full((1, D_MODEL)),
                  full((D_MODEL, D_MODEL)), mem, mem, full((D_MODEL, D_MODEL)), full((1, D_MODEL))],
        out_specs=[tok(D_MODEL), tok(D_MODEL)],
        out_shape=[jax.ShapeDtypeStruct((T, D_MODEL), F32), jax.ShapeDtypeStruct((T, D_MODEL), BF16)],
        scratch_shapes=[
            pltpu.VMEM((CONV_HALO + tm, B_WIDTH), F32),
            pltpu.VMEM((tm, B_WIDTH), BF16),
        ],
        compiler_params=pltpu.CompilerParams(
            dimension_semantics=("arbitrary",), vmem_limit_bytes=VMEM_LIMIT_BYTES),
        name="mid",
    )(x2d, x2d, a, gm, wg, cw, cb, lng, lnb, wa, wb, gc, wq, kc, vc, wo, gf)


def _ffn_kernel(tiles_per_seq, x_ref, h_ref, wg_ref, wu_ref, cw_ref, cb_ref, wd_ref, gfin_ref, y_ref,
                halo, gbuf, acc_ref):
    tm = x_ref.shape[0]

    @pl.when(pl.program_id(0) % tiles_per_seq == 0)
    def _():
        halo[...] = jnp.zeros(halo.shape, F32)

    h = h_ref[...]
    acc_ref[...] = x_ref[...]
    for c in range(D_FF // FF_CHUNK):
        cs = slice(c * FF_CHUNK, (c + 1) * FF_CHUNK)
        gate = _dot(h, wg_ref[:, cs])
        up = _dot(h, wu_ref[:, cs])
        gbuf[0:SUBLANES, :] = halo[:, cs]
        gbuf[SUBLANES:SUBLANES + tm, :] = gate
        halo[:, cs] = gate[tm - SUBLANES:, :]
        cw = cw_ref[:, cs]
        g = cb_ref[:, cs] + cw[2:3, :] * gate
        g = g + cw[1:2, :] * gbuf[SUBLANES - 1:SUBLANES - 1 + tm, :]
        g = g + cw[0:1, :] * gbuf[SUBLANES - 2:SUBLANES - 2 + tm, :]
        act = (g * jax.nn.sigmoid(g) * up).astype(BF16)
        acc_ref[...] += _dot(act, wd_ref[cs, :])
    y_ref[...] = _rmsnorm(acc_ref[...], gfin_ref[...])


def _ffn(x2, hf, wg, wu, cw, cb, wd, gfin, seq):
    T = x2.shape[0]
    tm = FFN_TILE
    full = lambda shape: pl.BlockSpec(shape, lambda i: (0,) * len(shape), pipeline_mode=pl.Buffered(1))
    tok = pl.BlockSpec((tm, D_MODEL), lambda i: (i, 0))
    return pl.pallas_call(
        functools.partial(_ffn_kernel, seq // tm),
        grid=(T // tm,),
        in_specs=[tok, tok, full((D_MODEL, D_FF)), full((D_MODEL, D_FF)), full((FFN_CONV_WIDTH, D_FF)),
                  full((1, D_FF)), full((D_FF, D_MODEL)), full((1, D_MODEL))],
        out_specs=tok,
        out_shape=jax.ShapeDtypeStruct((T, D_MODEL), F32),
        scratch_shapes=[
            pltpu.VMEM((SUBLANES, D_FF), F32),
            pltpu.VMEM((SUBLANES + tm, FF_CHUNK), F32),
            pltpu.VMEM((tm, D_MODEL), F32),
        ],
        compiler_params=pltpu.CompilerParams(
            dimension_semantics=("arbitrary",), vmem_limit_bytes=VMEM_LIMIT_BYTES),
        name="ffn",
    )(x2, hf, wg, wu, cw, cb, wd, gfin)


def kernel(x, mem, positions, norm_mix_g, w_in, w_out, conv_b_w, conv_b_b, ln_b_g, ln_b_b, norm_cross_g, norm_mem_g, w_q_cross, w_k_cross, w_v_cross, w_o_cross, norm_ffn_g, w_gate, w_up, ffn_conv_w, ffn_conv_b, w_down, norm_final_g):
    batch, seq, _ = x.shape
    n_mem = mem.shape[1]
    depth = w_in.shape[0]
    assert depth == 1 and seq % TOKEN_TILE == 0 and seq % FFN_TILE == 0 and seq % QUERY_TILE == 0

    row = lambda v: v.reshape(1, -1)
    x2d = x.reshape(batch * seq, D_MODEL)
    pos = positions.reshape(1, batch * seq)
    freq = (ROPE_THETA ** (-jnp.arange(0, ROT_DIM, 2, dtype=F32) / ROT_DIM)).reshape(ROT_HALF, 1)

    wi_ = w_in[0]
    c_q, c_k, c_v, c_qi, c_ki, c_wi = 0, A_WIDTH, A_WIDTH + 64, A_WIDTH + 128, A_WIDTH + 384, A_WIDTH + 448
    c_glu = c_wi + IDX_HEADS
    wt = jnp.concatenate([
        wi_[:, c_q:c_k], wi_[:, c_qi:c_ki], wi_[:, c_k:c_v], wi_[:, c_ki:c_wi], wi_[:, c_v:c_qi],
        wi_[:, c_wi:c_glu], jnp.zeros((D_MODEL, ROWS_T - ROW_WI - IDX_HEADS), F32)], axis=1).T.astype(BF16)
    wg = wi_[:, c_glu:].astype(BF16)

    qT, qiT, k, ki, vT, wiT = _in_proj(pos, x2d, row(norm_mix_g[0]), wt, freq)
    a_out = _dsa(qT, qiT, wiT, k, ki, vT, batch, seq)

    kc, vc = _mem_kv(mem.reshape(batch * n_mem, D_MODEL), row(norm_mem_g[0]),
                     w_k_cross[0].astype(BF16), w_v_cross[0].astype(BF16))
    wo_mix = w_out[0].astype(BF16)
    x2, hf = _mid(x2d, a_out, row(norm_mix_g[0]), wg, conv_b_w[0], row(conv_b_b[0]), row(ln_b_g[0]),
                  row(ln_b_b[0]), wo_mix[:A_WIDTH], wo_mix[A_WIDTH:], row(norm_cross_g[0]),
                  w_q_cross[0].astype(BF16), kc, vc, w_o_cross[0].astype(BF16), row(norm_ffn_g[0]), seq, n_mem)
    y = _ffn(x2, hf, w_gate[0].astype(BF16), w_up[0].astype(BF16), ffn_conv_w[0], row(ffn_conv_b[0]),
             w_down[0].astype(BF16), row(norm_final_g), seq)
    return y.reshape(batch, seq, D_MODEL)
```

```python
import functools

import jax
import jax.numpy as jnp
from jax import lax
from jax.experimental import pallas as pl
from jax.experimental.pallas import tpu as pltpu

F32 = jnp.float32
BF16 = jnp.bfloat16

D_MODEL = 1024
A_HEADS = 8
HEAD_DIM = 64
A_WIDTH = A_HEADS * HEAD_DIM
IDX_HEADS = 4
IDX_DIM = 64
TOPK_MAX = 256
B_WIDTH = D_MODEL - A_WIDTH
CONV_B_WIDTH = 31
ROPE_THETA = 500000.0
ROT_DIM = HEAD_DIM // 4
ROT_HALF = ROT_DIM // 2
CROSS_HEADS = 4
CROSS_HEAD_DIM = D_MODEL // CROSS_HEADS
D_FF = 2816
FFN_CONV_WIDTH = 3
EPS = 1e-6

SUBLANES = 8
TOKEN_TILE = 512
MID_TILE = 1024
QUERY_TILE = 512
CONV_HALO = 32
CONV_ROWS = 64
SCORE_ROWS = 512
COUNT_ROWS = 8
BF16_ROWS = 16
BF16_EXACT_INT = 256
FF_CHUNK = 256
VMEM_LIMIT_BYTES = 56 * 1024 * 1024

ROW_Q = 0
ROW_QI = ROW_Q + A_WIDTH
ROW_K = ROW_QI + IDX_HEADS * IDX_DIM
ROW_KI = ROW_K + HEAD_DIM
ROW_V = ROW_KI + IDX_DIM
ROW_WI = ROW_V + HEAD_DIM
ROWS_T = ROW_WI + 16

V_ROWS = HEAD_DIM + 16
LOG2_E = 1.4426950408889634

INT_MIN = -(2 ** 31)
KEY_NEG_INF = INT_MIN + 0x7FFFFF
NEG_INF = float("-inf")


def _rmsnorm(x, g):
    ms = jnp.mean(x * x, axis=-1, keepdims=True)
    return (x * lax.rsqrt(ms + EPS)) * g


def _dot(a, b):
    return jnp.dot(a, b, preferred_element_type=F32)


def _dot_nt(a, b):
    return lax.dot_general(a, b, (((1,), (1,)), ((), ())), preferred_element_type=F32)


def _in_proj_kernel(pos_ref, x_ref, g_ref, wt_ref, freq_ref, qT_ref, qiT_ref, k_ref, ki_ref, vT_ref, wiT_ref):
    tm = x_ref.shape[0]
    h = _rmsnorm(x_ref[...], g_ref[...]).astype(BF16)

    r = _dot_nt(wt_ref[...], h)
    ang = freq_ref[...] * pos_ref[...].astype(F32)
    cos = jnp.cos(ang)
    sin = jnp.sin(ang)

    def rope(xt, heads):
        x3 = xt.reshape(heads, HEAD_DIM, tm)
        x1 = x3[:, 0:ROT_HALF, :]
        x2 = x3[:, ROT_HALF:ROT_DIM, :]
        out = jnp.concatenate([x1 * cos - x2 * sin, x2 * cos + x1 * sin, x3[:, ROT_DIM:, :]], axis=1)
        return out.reshape(heads * HEAD_DIM, tm)

    qT_ref[...] = (rope(r[ROW_Q:ROW_QI], A_HEADS) * (HEAD_DIM ** -0.5 * LOG2_E)).astype(BF16)
    qiT_ref[...] = (rope(r[ROW_QI:ROW_K], IDX_HEADS) * (IDX_DIM ** -0.5)).astype(BF16)
    kk = rope(r[ROW_K:ROW_V], 2).T
    k_ref[...] = kk[:, :HEAD_DIM].astype(BF16)
    ki_ref[...] = kk[:, HEAD_DIM:].astype(BF16)
    ones_rows = (lax.broadcasted_iota(jnp.int32, (V_ROWS - HEAD_DIM, tm), 0) == 0).astype(F32)
    vT_ref[...] = jnp.concatenate([r[ROW_V:ROW_WI], ones_rows], axis=0).astype(BF16)
    wiT_ref[...] = r[ROW_WI:ROW_WI + SUBLANES] * (IDX_HEADS ** -0.5)


def _group_b_steps(x, g_ref, wg_ref, cw_ref, cb_ref, lng_ref, lnb_ref, ubuf, b_buf):
    tm = x.shape[0]
    h = _rmsnorm(x, g_ref[...]).astype(BF16)
    glu = _dot(h, wg_ref[...])
    u = glu[:, :B_WIDTH] * jax.nn.sigmoid(glu[:, B_WIDTH:])
    ubuf[CONV_HALO:CONV_HALO + tm, :] = u
    base = CONV_HALO - (CONV_B_WIDTH - 1)
    n_chunks = tm // CONV_ROWS

    def conv_half(c, ls):
        c0 = c * CONV_ROWS
        acc = jnp.broadcast_to(cb_ref[:, ls], (CONV_ROWS, B_WIDTH // 2))
        for r in range(SUBLANES):
            rows = CONV_ROWS + (SUBLANES if r else 0)
            q = None
            for j in range(CONV_B_WIDTH):
                if (base + j) % SUBLANES != r:
                    continue
                a0 = c0 + (base + j) // SUBLANES * SUBLANES
                term = ubuf[a0:a0 + rows, ls] * cw_ref[j:j + 1, ls]
                q = term if q is None else q + term
            acc = acc + q[r:r + CONV_ROWS, :]
        return acc

    pending = []

    def first_half(c):
        pending.append(conv_half(c, slice(0, B_WIDTH // 2)))

    def second_half(c):
        acc = jnp.concatenate([pending.pop(), conv_half(c, slice(B_WIDTH // 2, B_WIDTH))], axis=1)
        mu = jnp.mean(acc, axis=-1, keepdims=True)
        var = jnp.mean(jnp.square(acc - mu), axis=-1, keepdims=True)
        y = (acc - mu) * lax.rsqrt(var + EPS) * lng_ref[...] + lnb_ref[...]
        b_buf[c * CONV_ROWS:(c + 1) * CONV_ROWS, :] = (y * jax.nn.sigmoid(y)).astype(BF16)
        if c == n_chunks - 1:
            ubuf[0:CONV_HALO, :] = ubuf[tm:tm + CONV_HALO, :]

    return [functools.partial(f, c) for c in range(n_chunks) for f in (first_half, second_half)]


def _in_proj(pos, x2d, g, wt, freq):
    T = x2d.shape[0]
    tm = TOKEN_TILE
    full = lambda shape: pl.BlockSpec(shape, lambda i: (0,) * len(shape))
    return pl.pallas_call(
        _in_proj_kernel,
        grid=(T // tm,),
        in_specs=[
            pl.BlockSpec((1, tm), lambda i: (0, i)),
            pl.BlockSpec((tm, D_MODEL), lambda i: (i, 0)),
            full((1, D_MODEL)),
            full((ROWS_T, D_MODEL)),
            full((ROT_HALF, 1)),
        ],
        out_specs=[
            pl.BlockSpec((A_WIDTH, tm), lambda i: (0, i)),
            pl.BlockSpec((IDX_HEADS * IDX_DIM, tm), lambda i: (0, i)),
            pl.BlockSpec((tm, HEAD_DIM), lambda i: (i, 0)),
            pl.BlockSpec((tm, IDX_DIM), lambda i: (i, 0)),
            pl.BlockSpec((V_ROWS, tm), lambda i: (0, i)),
            pl.BlockSpec((SUBLANES, tm), lambda i: (0, i)),
        ],
        out_shape=[
            jax.ShapeDtypeStruct((A_WIDTH, T), BF16),
            jax.ShapeDtypeStruct((IDX_HEADS * IDX_DIM, T), BF16),
            jax.ShapeDtypeStruct((T, HEAD_DIM), BF16),
            jax.ShapeDtypeStruct((T, IDX_DIM), BF16),
            jax.ShapeDtypeStruct((V_ROWS, T), BF16),
            jax.ShapeDtypeStruct((SUBLANES, T), F32),
        ],
        compiler_params=pltpu.CompilerParams(
            dimension_semantics=("arbitrary",), vmem_limit_bytes=VMEM_LIMIT_BYTES),
        name="in_proj",
    )(pos, x2d, g, wt, freq)


def _key_to_f32(key):
    bits = jnp.where(key >= 0, key, key ^ jnp.int32(0x7FFFFFFF))
    return lax.bitcast_convert_type(bits, F32)


def _high_half(x):
    bits = lax.bitcast_convert_type(x, jnp.int32) & jnp.int32(-(2 ** 16))
    return lax.bitcast_convert_type(bits, F32).astype(BF16)


def _dsa_kernel(qT_ref, qiT_ref, wT_ref, k_ref, ki_ref, vT_ref, o_ref, sc_ref, sh_ref, lg_ref, acc_ref, m_ref,
                cm_ref):
    tq = qT_ref.shape[1]
    kc = tq
    qi_blk = pl.program_id(1)
    nchunk = qi_blk + 1

    row = lax.broadcasted_iota(jnp.int32, (kc, tq), 0)
    col = lax.broadcasted_iota(jnp.int32, (kc, tq), 1)
    causal = row <= col

    def chunk_start(c):
        return c * kc if isinstance(c, int) else pl.multiple_of(c * kc, kc)

    w = wT_ref[...]

    def score_body(c, carry):
        r0 = pl.multiple_of(c * SCORE_ROWS, SCORE_ROWS)
        ki_c = ki_ref[pl.ds(r0, SCORE_ROWS), :]
        s = jnp.zeros((SCORE_ROWS, tq), F32)
        for h in range(IDX_HEADS):
            d = _dot(ki_c, qiT_ref[h * IDX_DIM:(h + 1) * IDX_DIM, :])
            s = s + w[h:h + 1, :] * jnp.maximum(d, 0.0)
        sc_ref[pl.ds(r0, SCORE_ROWS), :] = s
        sh_ref[pl.ds(r0, SCORE_ROWS), :] = _high_half(s)
        return carry

    lax.fori_loop(0, nchunk * (kc // SCORE_ROWS), score_body, 0)
    d0 = chunk_start(qi_blk)
    diag = jnp.where(causal, sc_ref[pl.ds(d0, kc), :], NEG_INF)
    sc_ref[pl.ds(d0, kc), :] = diag
    sh_ref[pl.ds(d0, kc), :] = _high_half(diag)

    def count_ge16(thr):
        one = jnp.ones((), BF16)
        zero = jnp.zeros((), BF16)

        def body(c, acc):
            r0 = chunk_start(c)
            for g in range(kc // BF16_ROWS):
                blk = sh_ref[pl.ds(r0 + g * BF16_ROWS, BF16_ROWS), :]
                acc = acc + jnp.where(blk >= thr, one, zero)
            return acc
        acc = lax.fori_loop(0, nchunk, body, jnp.zeros((BF16_ROWS, tq), BF16))
        return acc.astype(F32).sum(axis=0, keepdims=True)

    def bit16_body(it, t):
        trial = t + lax.shift_left(jnp.int32(1), 15 - it)
        bits = jnp.where(trial >= 0, trial, trial ^ jnp.int32(0x7FFF))
        thr = lax.bitcast_convert_type(lax.shift_left(bits, 16), F32).astype(BF16)
        return jnp.where(count_ge16(thr) >= TOPK_MAX, trial, t)

    t16 = lax.fori_loop(0, 16, bit16_body, jnp.full((1, tq), -(2 ** 15), jnp.int32))

    def count_ge(thr):
        def body(c, acc):
            blk = sc_ref[pl.ds(chunk_start(c), kc), :]
            ge = (blk >= thr).astype(jnp.int32)
            return acc + ge.reshape(kc // COUNT_ROWS, COUNT_ROWS, tq).sum(axis=0)
        acc = lax.fori_loop(0, nchunk, body, jnp.zeros((COUNT_ROWS, tq), jnp.int32))
        return acc.sum(axis=0, keepdims=True)

    def bit_body(it, t):
        trial = t + lax.shift_left(jnp.int32(1), 15 - it)
        cnt = count_ge(_key_to_f32(trial))
        return jnp.where(cnt >= TOPK_MAX, trial, t)

    t = lax.fori_loop(0, 16, bit_body, lax.shift_left(t16, 16))
    tau = _key_to_f32(jnp.maximum(t, KEY_NEG_INF))

    def gt_body(c, acc):
        blk = sc_ref[pl.ds(chunk_start(c), kc), :]
        return acc + (blk > tau).astype(jnp.int32).reshape(kc // COUNT_ROWS, COUNT_ROWS, tq).sum(axis=0)

    n_gt = lax.fori_loop(0, nchunk, gt_body, jnp.zeros((COUNT_ROWS, tq), jnp.int32)).sum(axis=0, keepdims=True)
    need = (TOPK_MAX - n_gt).astype(F32)

    lower_tri = (row > col).astype(BF16)

    def bias_body(c, carry):
        r0 = chunk_start(c)
        counts = []
        for ls in (slice(0, tq // 2), slice(tq // 2, tq)):
            blk = sc_ref[pl.ds(r0, kc), ls]
            eq = blk == tau[:, ls]
            eqf = jnp.where(eq, 1.0, 0.0)
            rank = _dot(lower_tri, eqf.astype(BF16)) + carry[:, ls]
            sel = (blk > tau[:, ls]) | (eq & (rank < need[:, ls]))
            sc_ref[pl.ds(r0, kc), ls] = jnp.where(sel, 0.0, NEG_INF)
            counts.append(eqf.sum(axis=0, keepdims=True))
        return carry + jnp.concatenate(counts, axis=1)

    lax.fori_loop(0, nchunk, bias_body, jnp.zeros((1, tq), F32))
    sc_ref[pl.ds(d0, kc), :] = jnp.where(causal, sc_ref[pl.ds(d0, kc), :], NEG_INF)

    m_ref[...] = jnp.full(m_ref.shape, NEG_INF, F32)
    acc_ref[...] = jnp.zeros(acc_ref.shape, F32)
    units = [(h, ls) for h in range(A_HEADS) for ls in (slice(0, tq // 2), slice(tq // 2, tq))]

    def stage_chunk(c, consume_unit):
        r0 = chunk_start(c)
        k_c = k_ref[pl.ds(r0, kc), :]
        bias = sc_ref[pl.ds(r0, kc), :]
        cmax = [[], []]
        for u, (h, ls) in enumerate(units):
            consume_unit(h, ls)
            logit = _dot(k_c, qT_ref[h * HEAD_DIM:(h + 1) * HEAD_DIM, ls]) + bias[:, ls]
            lg_ref[h, :, ls] = logit
            cmax[u % 2].append(logit.max(axis=0, keepdims=True))
        cm_ref[...] = jnp.concatenate([jnp.concatenate(half, axis=0) for half in cmax], axis=1)

    def consumer(c):
        vT_c = vT_ref[:, pl.ds(chunk_start(c), kc)]
        m_old = m_ref[...]
        m_new = jnp.maximum(m_old, cm_ref[...])
        m_safe = jnp.where(m_new == NEG_INF, 0.0, m_new)
        alpha = jnp.exp2(m_old - m_safe)
        m_ref[...] = m_new

        def consume_unit(h, ls):
            p = jnp.exp2(lg_ref[h, :, ls] - m_safe[h:h + 1, ls]).astype(BF16)
            acc_ref[h, :, ls] = alpha[h:h + 1, ls] * acc_ref[h, :, ls] + _dot(vT_c, p)
        return consume_unit

    stage_chunk(0, lambda h, ls: None)

    def att_body(c, carry):
        stage_chunk(c, consumer(c - 1))
        return carry

    lax.fori_loop(1, nchunk, att_body, 0)
    last = consumer(nchunk - 1)
    for h, ls in units:
        last(h, ls)

    outs = []
    for h in range(A_HEADS):
        a = acc_ref[h]
        outs.append(a[:HEAD_DIM, :] * (1.0 / a[HEAD_DIM:HEAD_DIM + 1, :]))
    o_ref[...] = jnp.concatenate(outs, axis=0).T.astype(BF16)


def _dsa(qT, qiT, wiT, k, ki, vT, batch, seq):
    tq = QUERY_TILE
    nq = seq // tq
    T = batch * seq
    return pl.pallas_call(
        _dsa_kernel,
        grid=(batch, nq),
        in_specs=[
            pl.BlockSpec((A_WIDTH, tq), lambda b, i: (0, b * nq + i)),
            pl.BlockSpec((IDX_HEADS * IDX_DIM, tq), lambda b, i: (0, b * nq + i)),
            pl.BlockSpec((SUBLANES, tq), lambda b, i: (0, b * nq + i)),
            pl.BlockSpec((seq, HEAD_DIM), lambda b, i: (b, 0)),
            pl.BlockSpec((seq, IDX_DIM), lambda b, i: (b, 0)),
            pl.BlockSpec((V_ROWS, seq), lambda b, i: (0, b)),
        ],
        out_specs=pl.BlockSpec((tq, A_WIDTH), lambda b, i: (b * nq + i, 0)),
        out_shape=jax.ShapeDtypeStruct((T, A_WIDTH), BF16),
        scratch_shapes=[
            pltpu.VMEM((seq, tq), F32),
            pltpu.VMEM((seq, tq), BF16),
            pltpu.VMEM((A_HEADS, tq, tq), F32),
            pltpu.VMEM((A_HEADS, V_ROWS, tq), F32),
            pltpu.VMEM((A_HEADS, tq), F32),
            pltpu.VMEM((A_HEADS, tq), F32),
        ],
        compiler_params=pltpu.CompilerParams(
            dimension_semantics=("arbitrary", "arbitrary"), vmem_limit_bytes=VMEM_LIMIT_BYTES),
        name="dsa",
    )(qT, qiT, wiT, k, ki, vT)


def _mem_kv_kernel(mem_ref, g_ref, wk_ref, wv_ref, kc_ref, vc_ref):
    m = _rmsnorm(mem_ref[...], g_ref[...]).astype(BF16)
    kc_ref[...] = _dot(m, wk_ref[...]).astype(BF16)
    vc_ref[...] = _dot(m, wv_ref[...]).astype(BF16)


def _mem_kv(mem2d, g, wk, wv):
    M = mem2d.shape[0]
    tm = TOKEN_TILE
    full = lambda shape: pl.BlockSpec(shape, lambda i: (0,) * len(shape))
    return pl.pallas_call(
        _mem_kv_kernel,
        grid=(M // tm,),
        in_specs=[pl.BlockSpec((tm, D_MODEL), lambda i: (i, 0)), full((1, D_MODEL)),
                  full((D_MODEL, D_MODEL)), full((D_MODEL, D_MODEL))],
        out_specs=[pl.BlockSpec((tm, D_MODEL), lambda i: (i, 0))] * 2,
        out_shape=[jax.ShapeDtypeStruct((M, D_MODEL), BF16)] * 2,
        compiler_params=pltpu.CompilerParams(
            dimension_semantics=("arbitrary",), vmem_limit_bytes=VMEM_LIMIT_BYTES),
        name="mem_kv",
    )(mem2d, g, wk, wv)


def _mid_kernel(tiles_per_seq, xn_ref, x_ref, a_ref, gm_ref, wg_ref, cw_ref, cb_ref, lng_ref, lnb_ref, wa_ref,
                wb_ref, gc_ref, wq_ref, kc_ref, vc_ref, wo_ref, gf_ref, x2_ref, hf_ref, ubuf, b_buf):
    step = pl.program_id(0)

    @pl.when(step == 0)
    def _():
        b_buf[...] = jnp.zeros(b_buf.shape, BF16)

    @pl.when(step % tiles_per_seq == 0)
    def _():
        ubuf[0:CONV_HALO, :] = jnp.zeros((CONV_HALO, B_WIDTH), F32)

    x = x_ref[...]
    a = a_ref[...]
    b = b_buf[...]
    conv_chunks = _group_b_steps(xn_ref[...], gm_ref, wg_ref, cw_ref, cb_ref, lng_ref, lnb_ref, ubuf, b_buf)
    stages_left = [6 * CROSS_HEADS]

    def run_conv_chunks():
        for _ in range(-(-len(conv_chunks) // stages_left[0])):
            conv_chunks.pop(0)()
        stages_left[0] -= 1

    head_cols = [slice(h * CROSS_HEAD_DIM, (h + 1) * CROSS_HEAD_DIM) for h in range(CROSS_HEADS)]
    x1_cols = []
    for cs in head_cols:
        part = x[:, cs] + _dot(a, wa_ref[:, cs])
        run_conv_chunks()
        x1_cols.append(part + _dot(b, wb_ref[:, cs]))
        run_conv_chunks()
    x1 = jnp.concatenate(x1_cols, axis=-1)
    hq = _rmsnorm(x1, gc_ref[...]).astype(BF16)
    heads = []
    for hs in head_cols:
        qc = (_dot(hq, wq_ref[:, hs]) * (CROSS_HEAD_DIM ** -0.5)).astype(BF16)
        run_conv_chunks()
        logit = _dot_nt(qc, kc_ref[:, hs])
        p = jnp.exp(logit - logit.max(axis=-1, keepdims=True))
        p = p * (1.0 / p.sum(axis=-1, keepdims=True))
        run_conv_chunks()
        heads.append(_dot(p.astype(BF16), vc_ref[:, hs]))
        run_conv_chunks()
    oc = jnp.concatenate(heads, axis=-1).astype(BF16)
    x2_cols = []
    for cs in head_cols:
        x2_cols.append(x1[:, cs] + _dot(oc, wo_ref[:, cs]))
        run_conv_chunks()
    assert not conv_chunks
    x2 = jnp.concatenate(x2_cols, axis=-1)
    x2_ref[...] = x2
    hf_ref[...] = _rmsnorm(x2, gf_ref[...]).astype(BF16)


def _mid(x2d, a, gm, wg, cw, cb, lng, lnb, wa, wb, gc, wq, kc, vc, wo, gf, seq, n_mem):
    T = x2d.shape[0]
    tm = MID_TILE
    n_tiles = T // tm
    tiles_per_seq = seq // tm
    full = lambda shape: pl.BlockSpec(shape, lambda i: (0,) * len(shape), pipeline_mode=pl.Buffered(1))
    prev = lambda i: jnp.maximum(i - 1, 0)
    tok = lambda width: pl.BlockSpec((tm, width), lambda i: (prev(i), 0))
    mem = pl.BlockSpec((n_mem, D_MODEL), lambda i: (prev(i) // tiles_per_seq, 0))
    return pl.pallas_call(
        functools.partial(_mid_kernel, tiles_per_seq),
        grid=(n_tiles + 1,),
        in_specs=[pl.BlockSpec((tm, D_MODEL), lambda i: (jnp.minimum(i, n_tiles - 1), 0)),
                  tok(D_MODEL), tok(A_WIDTH), full((1, D_MODEL)), full((D_MODEL, 2 * B_WIDTH)),
                  full((CONV_B_WIDTH, B_WIDTH)), full((1, B_WIDTH)), full((1, B_WIDTH)), full((1, B_WIDTH)),
                  full((A_WIDTH, D_MODEL)), full((B_WIDTH, D_MODEL)), full((1, D_MODEL)),
                  full((D_MODEL, D_MODEL)), mem, mem, full((D_MODEL, D_MODEL)), full((1, D_MODEL))],
        out_specs=[tok(D_MODEL), tok(D_MODEL)],
        out_shape=[jax.ShapeDtypeStruct((T, D_MODEL), F32), jax.ShapeDtypeStruct((T, D_MODEL), BF16)],
        scratch_shapes=[
            pltpu.VMEM((CONV_HALO + tm, B_WIDTH), F32),
            pltpu.VMEM((tm, B_WIDTH), BF16),
        ],
        compiler_params=pltpu.CompilerParams(
            dimension_semantics=("arbitrary",), vmem_limit_bytes=VMEM_LIMIT_BYTES),
        name="mid",
    )(x2d, x2d, a, gm, wg, cw, cb, lng, lnb, wa, wb, gc, wq, kc, vc, wo, gf)


def _ffn_kernel(tiles_per_seq, x_ref, h_ref, wg_ref, wu_ref, cw_ref, cb_ref, wd_ref, gfin_ref, y_ref,
                halo, gbuf, act_ref):
    tm = x_ref.shape[0]

    @pl.when(pl.program_id(0) % tiles_per_seq == 0)
    def _():
        halo[...] = jnp.zeros(halo.shape, F32)

    h = h_ref[...]
    for c in range(D_FF // FF_CHUNK):
        cs = slice(c * FF_CHUNK, (c + 1) * FF_CHUNK)
        gb = gbuf.at[c % 2]
        gate = _dot(h, wg_ref[:, cs])
        up = _dot(h, wu_ref[:, cs])
        gb[0:SUBLANES, :] = halo[:, cs]
        gb[SUBLANES:SUBLANES + tm, :] = gate
        halo[:, cs] = gate[tm - SUBLANES:, :]
        cw = cw_ref[:, cs]
        g = cb_ref[:, cs] + cw[2:3, :] * gate
        g = g + cw[1:2, :] * gb[SUBLANES - 1:SUBLANES - 1 + tm, :]
        g = g + cw[0:1, :] * gb[SUBLANES - 2:SUBLANES - 2 + tm, :]
        act_ref[:, cs] = (g * jax.nn.sigmoid(g) * up).astype(BF16)
    y_ref[...] = _rmsnorm(x_ref[...] + _dot(act_ref[...], wd_ref[...]), gfin_ref[...])


def _ffn(x2, hf, wg, wu, cw, cb, wd, gfin, seq):
    T = x2.shape[0]
    tm = TOKEN_TILE
    full = lambda shape: pl.BlockSpec(shape, lambda i: (0,) * len(shape), pipeline_mode=pl.Buffered(1))
    tok = pl.BlockSpec((tm, D_MODEL), lambda i: (i, 0))
    return pl.pallas_call(
        functools.partial(_ffn_kernel, seq // tm),
        grid=(T // tm,),
        in_specs=[tok, tok, full((D_MODEL, D_FF)), full((D_MODEL, D_FF)), full((FFN_CONV_WIDTH, D_FF)),
                  full((1, D_FF)), full((D_FF, D_MODEL)), full((1, D_MODEL))],
        out_specs=tok,
        out_shape=jax.ShapeDtypeStruct((T, D_MODEL), F32),
        scratch_shapes=[
            pltpu.VMEM((SUBLANES, D_FF), F32),
            pltpu.VMEM((2, SUBLANES + tm, FF_CHUNK), F32),
            pltpu.VMEM((tm, D_FF), BF16),
        ],
        compiler_params=pltpu.CompilerParams(
            dimension_semantics=("arbitrary",), vmem_limit_bytes=VMEM_LIMIT_BYTES),
        name="ffn",
    )(x2, hf, wg, wu, cw, cb, wd, gfin)


def kernel(x, mem, positions, norm_mix_g, w_in, w_out, conv_b_w, conv_b_b, ln_b_g, ln_b_b, norm_cross_g, norm_mem_g, w_q_cross, w_k_cross, w_v_cross, w_o_cross, norm_ffn_g, w_gate, w_up, ffn_conv_w, ffn_conv_b, w_down, norm_final_g):
    batch, seq, _ = x.shape
    n_mem = mem.shape[1]
    depth = w_in.shape[0]
    assert depth == 1 and seq % TOKEN_TILE == 0 and seq % MID_TILE == 0 and seq % QUERY_TILE == 0
    assert seq // BF16_ROWS <= BF16_EXACT_INT

    row = lambda v: v.reshape(1, -1)
    x2d = x.reshape(batch * seq, D_MODEL)
    pos = positions.reshape(1, batch * seq)
    freq = (ROPE_THETA ** (-jnp.arange(0, ROT_DIM, 2, dtype=F32) / ROT_DIM)).reshape(ROT_HALF, 1)

    wi_ = w_in[0]
    c_q, c_k, c_v, c_qi, c_ki, c_wi = 0, A_WIDTH, A_WIDTH + 64, A_WIDTH + 128, A_WIDTH + 384, A_WIDTH + 448
    c_glu = c_wi + IDX_HEADS
    wt = jnp.concatenate([
        wi_[:, c_q:c_k], wi_[:, c_qi:c_ki], wi_[:, c_k:c_v], wi_[:, c_ki:c_wi], wi_[:, c_v:c_qi],
        wi_[:, c_wi:c_glu], jnp.zeros((D_MODEL, ROWS_T - ROW_WI - IDX_HEADS), F32)], axis=1).T.astype(BF16)
    wg = wi_[:, c_glu:].astype(BF16)

    qT, qiT, k, ki, vT, wiT = _in_proj(pos, x2d, row(norm_mix_g[0]), wt, freq)
    a_out = _dsa(qT, qiT, wiT, k, ki, vT, batch, seq)

    kc, vc = _mem_kv(mem.reshape(batch * n_mem, D_MODEL), row(norm_mem_g[0]),
                     w_k_cross[0].astype(BF16), w_v_cross[0].astype(BF16))
    wo_mix = w_out[0].astype(BF16)
    x2, hf = _mid(x2d, a_out, row(norm_mix_g[0]), wg, conv_b_w[0], row(conv_b_b[0]), row(ln_b_g[0]),
                  row(ln_b_b[0]), wo_mix[:A_WIDTH], wo_mix[A_WIDTH:], row(norm_cross_g[0]),
                  w_q_cross[0].astype(BF16), kc, vc, w_o_cross[0].astype(BF16), row(norm_ffn_g[0]), seq, n_mem)
    y = _ffn(x2, hf, w_gate[0].astype(BF16), w_up[0].astype(BF16), ffn_conv_w[0], row(ffn_conv_b[0]),
             w_down[0].astype(BF16), row(norm_final_g), seq)
    return y.reshape(batch, seq, D_MODEL)
```

```python
import functools

import jax
import jax.numpy as jnp
from jax import lax
from jax.experimental import pallas as pl
from jax.experimental.pallas import tpu as pltpu

F32 = jnp.float32
BF16 = jnp.bfloat16

D_MODEL = 1024
A_HEADS = 8
HEAD_DIM = 64
A_WIDTH = A_HEADS * HEAD_DIM
IDX_HEADS = 4
IDX_DIM = 64
TOPK_MAX = 256
B_WIDTH = D_MODEL - A_WIDTH
CONV_B_WIDTH = 31
ROPE_THETA = 500000.0
ROT_DIM = HEAD_DIM // 4
ROT_HALF = ROT_DIM // 2
CROSS_HEADS = 4
CROSS_HEAD_DIM = D_MODEL // CROSS_HEADS
D_FF = 2816
FFN_CONV_WIDTH = 3
EPS = 1e-6

SUBLANES = 8
LANES = 128
TOKEN_TILE = 512
MID_TILE = 1024
QUERY_TILE = 512
CONV_HALO = 32
CONV_ROWS = 64
SCORE_ROWS = 512
COUNT_ROWS = 8
BF16_ROWS = 16
BF16_EXACT_INT = 256
FF_CHUNK = 256
VMEM_LIMIT_BYTES = 56 * 1024 * 1024

ROW_Q = 0
ROW_QI = ROW_Q + A_WIDTH
ROW_K = ROW_QI + IDX_HEADS * IDX_DIM
ROW_KI = ROW_K + HEAD_DIM
ROW_V = ROW_KI + IDX_DIM
ROW_WI = ROW_V + HEAD_DIM
ROWS_T = ROW_WI + 16

V_ROWS = HEAD_DIM + 16
LOG2_E = 1.4426950408889634

INT_MIN = -(2 ** 31)
KEY_NEG_INF = INT_MIN + 0x7FFFFF
NEG_INF = float("-inf")


def _rmsnorm(x, g):
    ms = jnp.mean(x * x, axis=-1, keepdims=True)
    return (x * lax.rsqrt(ms + EPS)) * g


def _dot(a, b):
    return jnp.dot(a, b, preferred_element_type=F32)


def _dot_nt(a, b):
    return lax.dot_general(a, b, (((1,), (1,)), ((), ())), preferred_element_type=F32)


def _in_proj_kernel(pos_ref, x_ref, g_ref, wt_ref, freq_ref, qT_ref, qiT_ref, k_ref, ki_ref, vT_ref, wiT_ref):
    tm = x_ref.shape[0]
    h = _rmsnorm(x_ref[...], g_ref[...]).astype(BF16)

    r = _dot_nt(wt_ref[...], h)
    ang = freq_ref[...] * pos_ref[...].astype(F32)
    cos = jnp.cos(ang)
    sin = jnp.sin(ang)

    def rope(xt, heads):
        x3 = xt.reshape(heads, HEAD_DIM, tm)
        x1 = x3[:, 0:ROT_HALF, :]
        x2 = x3[:, ROT_HALF:ROT_DIM, :]
        out = jnp.concatenate([x1 * cos - x2 * sin, x2 * cos + x1 * sin, x3[:, ROT_DIM:, :]], axis=1)
        return out.reshape(heads * HEAD_DIM, tm)

    qT_ref[...] = (rope(r[ROW_Q:ROW_QI], A_HEADS) * (HEAD_DIM ** -0.5 * LOG2_E)).astype(BF16)
    qiT_ref[...] = (rope(r[ROW_QI:ROW_K], IDX_HEADS) * (IDX_DIM ** -0.5)).astype(BF16)
    kk = rope(r[ROW_K:ROW_V], 2).T
    k_ref[...] = kk[:, :HEAD_DIM].astype(BF16)
    ki_ref[...] = kk[:, HEAD_DIM:].astype(BF16)
    ones_rows = (lax.broadcasted_iota(jnp.int32, (V_ROWS - HEAD_DIM, tm), 0) == 0).astype(F32)
    vT_ref[...] = jnp.concatenate([r[ROW_V:ROW_WI], ones_rows], axis=0).astype(BF16)
    wiT_ref[...] = r[ROW_WI:ROW_WI + SUBLANES] * (IDX_HEADS ** -0.5)


def _group_b_steps(x, g_ref, wg_ref, cw_ref, cb_ref, lng_ref, lnb_ref, ubuf, b_buf):
    tm = x.shape[0]
    h = _rmsnorm(x, g_ref[...]).astype(BF16)
    glu = _dot(h, wg_ref[...])
    u = glu[:, :B_WIDTH] * jax.nn.sigmoid(glu[:, B_WIDTH:])
    ubuf[CONV_HALO:CONV_HALO + tm, :] = u
    base = CONV_HALO - (CONV_B_WIDTH - 1)
    n_chunks = tm // CONV_ROWS

    def conv_half(c, ls):
        c0 = c * CONV_ROWS
        acc = jnp.broadcast_to(cb_ref[:, ls], (CONV_ROWS, B_WIDTH // 2))
        for r in range(SUBLANES):
            rows = CONV_ROWS + (SUBLANES if r else 0)
            q = None
            for j in range(CONV_B_WIDTH):
                if (base + j) % SUBLANES != r:
                    continue
                a0 = c0 + (base + j) // SUBLANES * SUBLANES
                term = ubuf[a0:a0 + rows, ls] * cw_ref[j:j + 1, ls]
                q = term if q is None else q + term
            acc = acc + q[r:r + CONV_ROWS, :]
        return acc

    pending = []

    def first_half(c):
        pending.append(conv_half(c, slice(0, B_WIDTH // 2)))

    def second_half(c):
        acc = jnp.concatenate([pending.pop(), conv_half(c, slice(B_WIDTH // 2, B_WIDTH))], axis=1)
        mu = jnp.mean(acc, axis=-1, keepdims=True)
        var = jnp.mean(jnp.square(acc - mu), axis=-1, keepdims=True)
        y = (acc - mu) * lax.rsqrt(var + EPS) * lng_ref[...] + lnb_ref[...]
        b_buf[c * CONV_ROWS:(c + 1) * CONV_ROWS, :] = (y * jax.nn.sigmoid(y)).astype(BF16)
        if c == n_chunks - 1:
            ubuf[0:CONV_HALO, :] = ubuf[tm:tm + CONV_HALO, :]

    return [functools.partial(f, c) for c in range(n_chunks) for f in (first_half, second_half)]


def _in_proj(pos, x2d, g, wt, freq):
    T = x2d.shape[0]
    tm = TOKEN_TILE
    full = lambda shape: pl.BlockSpec(shape, lambda i: (0,) * len(shape))
    return pl.pallas_call(
        _in_proj_kernel,
        grid=(T // tm,),
        in_specs=[
            pl.BlockSpec((1, tm), lambda i: (0, i)),
            pl.BlockSpec((tm, D_MODEL), lambda i: (i, 0)),
            full((1, D_MODEL)),
            full((ROWS_T, D_MODEL)),
            full((ROT_HALF, 1)),
        ],
        out_specs=[
            pl.BlockSpec((A_WIDTH, tm), lambda i: (0, i)),
            pl.BlockSpec((IDX_HEADS * IDX_DIM, tm), lambda i: (0, i)),
            pl.BlockSpec((tm, HEAD_DIM), lambda i: (i, 0)),
            pl.BlockSpec((tm, IDX_DIM), lambda i: (i, 0)),
            pl.BlockSpec((V_ROWS, tm), lambda i: (0, i)),
            pl.BlockSpec((SUBLANES, tm), lambda i: (0, i)),
        ],
        out_shape=[
            jax.ShapeDtypeStruct((A_WIDTH, T), BF16),
            jax.ShapeDtypeStruct((IDX_HEADS * IDX_DIM, T), BF16),
            jax.ShapeDtypeStruct((T, HEAD_DIM), BF16),
            jax.ShapeDtypeStruct((T, IDX_DIM), BF16),
            jax.ShapeDtypeStruct((V_ROWS, T), BF16),
            jax.ShapeDtypeStruct((SUBLANES, T), F32),
        ],
        compiler_params=pltpu.CompilerParams(
            dimension_semantics=("arbitrary",), vmem_limit_bytes=VMEM_LIMIT_BYTES),
        name="in_proj",
    )(pos, x2d, g, wt, freq)


def _key_to_f32(key):
    bits = jnp.where(key >= 0, key, key ^ jnp.int32(0x7FFFFFFF))
    return lax.bitcast_convert_type(bits, F32)


def _high_half(x):
    bits = lax.bitcast_convert_type(x, jnp.int32) & jnp.int32(-(2 ** 16))
    return lax.bitcast_convert_type(bits, F32).astype(BF16)


def _dsa_kernel(qT_ref, qiT_ref, wT_ref, k_ref, ki_ref, vT_ref, o_ref, sc_ref, sh_ref, lg_ref, acc_ref, m_ref,
                cm_ref):
    tq = qT_ref.shape[1]
    kc = tq
    qi_blk = pl.program_id(1)
    nchunk = qi_blk + 1

    row = lax.broadcasted_iota(jnp.int32, (kc, tq), 0)
    col = lax.broadcasted_iota(jnp.int32, (kc, tq), 1)
    causal = row <= col

    def chunk_start(c):
        return c * kc if isinstance(c, int) else pl.multiple_of(c * kc, kc)

    w = wT_ref[...]

    def score_body(c, carry):
        r0 = pl.multiple_of(c * SCORE_ROWS, SCORE_ROWS)
        ki_c = ki_ref[pl.ds(r0, SCORE_ROWS), :]
        for ls in (slice(0, tq // 2), slice(tq // 2, tq)):
            s = jnp.zeros((SCORE_ROWS, tq // 2), F32)
            for h in range(IDX_HEADS):
                d = _dot(ki_c, qiT_ref[h * IDX_DIM:(h + 1) * IDX_DIM, ls])
                s = s + w[h:h + 1, ls] * jnp.maximum(d, 0.0)
            sc_ref[pl.ds(r0, SCORE_ROWS), ls] = s
            sh_ref[pl.ds(r0, SCORE_ROWS), ls] = _high_half(s)
        return carry

    lax.fori_loop(0, nchunk * (kc // SCORE_ROWS), score_body, 0)
    d0 = chunk_start(qi_blk)
    diag = jnp.where(causal, sc_ref[pl.ds(d0, kc), :], NEG_INF)
    sc_ref[pl.ds(d0, kc), :] = diag
    sh_ref[pl.ds(d0, kc), :] = _high_half(diag)

    lane_tiles = [slice(j * LANES, (j + 1) * LANES) for j in range(tq // LANES)]

    def count_ge16(thr):
        one = jnp.ones((), BF16)
        zero = jnp.zeros((), BF16)

        def body(c, acc):
            r0 = chunk_start(c)
            for g in range(kc // BF16_ROWS):
                blk = sh_ref[pl.ds(r0 + g * BF16_ROWS, BF16_ROWS), :]
                acc = acc + jnp.where(blk >= thr, one, zero)
            return acc
        acc = lax.fori_loop(0, nchunk, body, jnp.zeros((BF16_ROWS, tq), BF16))
        return acc.astype(F32).sum(axis=0, keepdims=True)

    def bit16_body(it, t):
        trial = t + lax.shift_left(jnp.int32(1), 15 - it)
        bits = jnp.where(trial >= 0, trial, trial ^ jnp.int32(0x7FFF))
        thr = lax.bitcast_convert_type(lax.shift_left(bits, 16), F32).astype(BF16)
        return jnp.where(count_ge16(thr) >= TOPK_MAX, trial, t)

    t16 = lax.fori_loop(0, 16, bit16_body, jnp.full((1, tq), -(2 ** 15), jnp.int32))

    def count_ge(thr):
        def body(c, acc):
            blk = sc_ref[pl.ds(chunk_start(c), kc), :]
            ge = (blk >= thr).astype(jnp.int32)
            return acc + ge.reshape(kc // COUNT_ROWS, COUNT_ROWS, tq).sum(axis=0)
        acc = lax.fori_loop(0, nchunk - 1, body, jnp.zeros((COUNT_ROWS, tq), jnp.int32))
        parts = []
        for j, ls in enumerate(lane_tiles):
            rows = (j + 1) * LANES
            ge = (sc_ref[pl.ds(d0, rows), ls] >= thr[:, ls]).astype(jnp.int32)
            parts.append(acc[:, ls] + ge.reshape(rows // COUNT_ROWS, COUNT_ROWS, LANES).sum(axis=0))
        acc = jnp.concatenate(parts, axis=1)
        return acc.sum(axis=0, keepdims=True)

    def bit_body(it, t):
        trial = t + lax.shift_left(jnp.int32(1), 15 - it)
        cnt = count_ge(_key_to_f32(trial))
        return jnp.where(cnt >= TOPK_MAX, trial, t)

    t = lax.fori_loop(0, 16, bit_body, lax.shift_left(t16, 16))
    tau = _key_to_f32(jnp.maximum(t, KEY_NEG_INF))

    def gt_body(c, acc):
        blk = sc_ref[pl.ds(chunk_start(c), kc), :]
        return acc + (blk > tau).astype(jnp.int32).reshape(kc // COUNT_ROWS, COUNT_ROWS, tq).sum(axis=0)

    n_gt = lax.fori_loop(0, nchunk, gt_body, jnp.zeros((COUNT_ROWS, tq), jnp.int32)).sum(axis=0, keepdims=True)
    need = (TOPK_MAX - n_gt).astype(F32)

    lower_tri = (row > col).astype(BF16)

    def bias_body(c, carry):
        r0 = chunk_start(c)
        counts = []
        for ls in (slice(0, tq // 2), slice(tq // 2, tq)):
            blk = sc_ref[pl.ds(r0, kc), ls]
            eq = blk == tau[:, ls]
            eqf = jnp.where(eq, 1.0, 0.0)
            rank = _dot(lower_tri, eqf.astype(BF16)) + carry[:, ls]
            sel = (blk > tau[:, ls]) | (eq & (rank < need[:, ls]))
            sc_ref[pl.ds(r0, kc), ls] = jnp.where(sel, 0.0, NEG_INF)
            counts.append(eqf.sum(axis=0, keepdims=True))
        return carry + jnp.concatenate(counts, axis=1)

    lax.fori_loop(0, nchunk, bias_body, jnp.zeros((1, tq), F32))
    sc_ref[pl.ds(d0, kc), :] = jnp.where(causal, sc_ref[pl.ds(d0, kc), :], NEG_INF)

    m_ref[...] = jnp.full(m_ref.shape, NEG_INF, F32)
    acc_ref[...] = jnp.zeros(acc_ref.shape, F32)
    units = [(h, ls) for h in range(A_HEADS) for ls in (slice(0, tq // 2), slice(tq // 2, tq))]

    def stage_chunk(c, consume_unit):
        r0 = chunk_start(c)
        k_c = k_ref[pl.ds(r0, kc), :]
        bias = sc_ref[pl.ds(r0, kc), :]
        cmax = [[], []]
        for u, (h, ls) in enumerate(units):
            consume_unit(h, ls)
            logit = _dot(k_c, qT_ref[h * HEAD_DIM:(h + 1) * HEAD_DIM, ls]) + bias[:, ls]
            lg_ref[h, :, ls] = logit
            cmax[u % 2].append(logit.max(axis=0, keepdims=True))
        cm_ref[...] = jnp.concatenate([jnp.concatenate(half, axis=0) for half in cmax], axis=1)

    def consumer(c):
        vT_c = vT_ref[:, pl.ds(chunk_start(c), kc)]
        m_old = m_ref[...]
        m_new = jnp.maximum(m_old, cm_ref[...])
        m_safe = jnp.where(m_new == NEG_INF, 0.0, m_new)
        alpha = jnp.exp2(m_old - m_safe)
        m_ref[...] = m_new

        def consume_unit(h, ls):
            p = jnp.exp2(lg_ref[h, :, ls] - m_safe[h:h + 1, ls]).astype(BF16)
            acc_ref[h, :, ls] = alpha[h:h + 1, ls] * acc_ref[h, :, ls] + _dot(vT_c, p)
        return consume_unit

    stage_chunk(0, lambda h, ls: None)

    def att_body(c, carry):
        stage_chunk(c, consumer(c - 1))
        return carry

    lax.fori_loop(1, nchunk, att_body, 0)
    last = consumer(nchunk - 1)
    for h, ls in units:
        last(h, ls)

    outs = []
    for h in range(A_HEADS):
        a = acc_ref[h]
        outs.append(a[:HEAD_DIM, :] * (1.0 / a[HEAD_DIM:HEAD_DIM + 1, :]))
    o_ref[...] = jnp.concatenate(outs, axis=0).T.astype(BF16)


def _dsa(qT, qiT, wiT, k, ki, vT, batch, seq):
    tq = QUERY_TILE
    nq = seq // tq
    T = batch * seq
    return pl.pallas_call(
        _dsa_kernel,
        grid=(batch, nq),
        in_specs=[
            pl.BlockSpec((A_WIDTH, tq), lambda b, i: (0, b * nq + i)),
            pl.BlockSpec((IDX_HEADS * IDX_DIM, tq), lambda b, i: (0, b * nq + i)),
            pl.BlockSpec((SUBLANES, tq), lambda b, i: (0, b * nq + i)),
            pl.BlockSpec((seq, HEAD_DIM), lambda b, i: (b, 0)),
            pl.BlockSpec((seq, IDX_DIM), lambda b, i: (b, 0)),
            pl.BlockSpec((V_ROWS, seq), lambda b, i: (0, b)),
        ],
        out_specs=pl.BlockSpec((tq, A_WIDTH), lambda b, i: (b * nq + i, 0)),
        out_shape=jax.ShapeDtypeStruct((T, A_WIDTH), BF16),
        scratch_shapes=[
            pltpu.VMEM((seq, tq), F32),
            pltpu.VMEM((seq, tq), BF16),
            pltpu.VMEM((A_HEADS, tq, tq), F32),
            pltpu.VMEM((A_HEADS, V_ROWS, tq), F32),
            pltpu.VMEM((A_HEADS, tq), F32),
            pltpu.VMEM((A_HEADS, tq), F32),
        ],
        compiler_params=pltpu.CompilerParams(
            dimension_semantics=("arbitrary", "arbitrary"), vmem_limit_bytes=VMEM_LIMIT_BYTES),
        name="dsa",
    )(qT, qiT, wiT, k, ki, vT)


def _mem_kv_kernel(mem_ref, g_ref, wk_ref, wv_ref, kc_ref, vc_ref):
    m = _rmsnorm(mem_ref[...], g_ref[...]).astype(BF16)
    kc_ref[...] = _dot(m, wk_ref[...]).astype(BF16)
    vc_ref[...] = _dot(m, wv_ref[...]).astype(BF16)


def _mem_kv(mem2d, g, wk, wv):
    M = mem2d.shape[0]
    tm = TOKEN_TILE
    full = lambda shape: pl.BlockSpec(shape, lambda i: (0,) * len(shape))
    return pl.pallas_call(
        _mem_kv_kernel,
        grid=(M // tm,),
        in_specs=[pl.BlockSpec((tm, D_MODEL), lambda i: (i, 0)), full((1, D_MODEL)),
                  full((D_MODEL, D_MODEL)), full((D_MODEL, D_MODEL))],
        out_specs=[pl.BlockSpec((tm, D_MODEL), lambda i: (i, 0))] * 2,
        out_shape=[jax.ShapeDtypeStruct((M, D_MODEL), BF16)] * 2,
        compiler_params=pltpu.CompilerParams(
            dimension_semantics=("arbitrary",), vmem_limit_bytes=VMEM_LIMIT_BYTES),
        name="mem_kv",
    )(mem2d, g, wk, wv)


def _mid_kernel(tiles_per_seq, xn_ref, x_ref, a_ref, gm_ref, wg_ref, cw_ref, cb_ref, lng_ref, lnb_ref, wa_ref,
                wb_ref, gc_ref, wq_ref, kc_ref, vc_ref, wo_ref, gf_ref, x2_ref, hf_ref, ubuf, b_buf):
    step = pl.program_id(0)

    @pl.when(step == 0)
    def _():
        b_buf[...] = jnp.zeros(b_buf.shape, BF16)

    @pl.when(step % tiles_per_seq == 0)
    def _():
        ubuf[0:CONV_HALO, :] = jnp.zeros((CONV_HALO, B_WIDTH), F32)

    x = x_ref[...]
    a = a_ref[...]
    b = b_buf[...]
    conv_chunks = _group_b_steps(xn_ref[...], gm_ref, wg_ref, cw_ref, cb_ref, lng_ref, lnb_ref, ubuf, b_buf)
    stages_left = [6 * CROSS_HEADS]

    def run_conv_chunks():
        for _ in range(-(-len(conv_chunks) // stages_left[0])):
            conv_chunks.pop(0)()
        stages_left[0] -= 1

    head_cols = [slice(h * CROSS_HEAD_DIM, (h + 1) * CROSS_HEAD_DIM) for h in range(CROSS_HEADS)]
    x1_cols = []
    for cs in head_cols:
        part = x[:, cs] + _dot(a, wa_ref[:, cs])
        run_conv_chunks()
        x1_cols.append(part + _dot(b, wb_ref[:, cs]))
        run_conv_chunks()
    x1 = jnp.concatenate(x1_cols, axis=-1)
    hq = _rmsnorm(x1, gc_ref[...]).astype(BF16)
    heads = []
    for hs in head_cols:
        qc = (_dot(hq, wq_ref[:, hs]) * (CROSS_HEAD_DIM ** -0.5)).astype(BF16)
        run_conv_chunks()
        logit = _dot_nt(qc, kc_ref[:, hs])
        p = jnp.exp(logit - logit.max(axis=-1, keepdims=True))
        p = p * (1.0 / p.sum(axis=-1, keepdims=True))
        run_conv_chunks()
        heads.append(_dot(p.astype(BF16), vc_ref[:, hs]))
        run_conv_chunks()
    oc = jnp.concatenate(heads, axis=-1).astype(BF16)
    x2_cols = []
    for cs in head_cols:
        x2_cols.append(x1[:, cs] + _dot(oc, wo_ref[:, cs]))
        run_conv_chunks()
    assert not conv_chunks
    x2 = jnp.concatenate(x2_cols, axis=-1)
    x2_ref[...] = x2
    hf_ref[...] = _rmsnorm(x2, gf_ref[...]).astype(BF16)


def _mid(x2d, a, gm, wg, cw, cb, lng, lnb, wa, wb, gc, wq, kc, vc, wo, gf, seq, n_mem):
    T = x2d.shape[0]
    tm = MID_TILE
    n_tiles = T // tm
    tiles_per_seq = seq // tm
    full = lambda shape: pl.BlockSpec(shape, lambda i: (0,) * len(shape), pipeline_mode=pl.Buffered(1))
    prev = lambda i: jnp.maximum(i - 1, 0)
    tok = lambda width: pl.BlockSpec((tm, width), lambda i: (prev(i), 0))
    mem = pl.BlockSpec((n_mem, D_MODEL), lambda i: (prev(i) // tiles_per_seq, 0))
    return pl.pallas_call(
        functools.partial(_mid_kernel, tiles_per_seq),
        grid=(n_tiles + 1,),
        in_specs=[pl.BlockSpec((tm, D_MODEL), lambda i: (jnp.minimum(i, n_tiles - 1), 0)),
                  tok(D_MODEL), tok(A_WIDTH), full((1, D_MODEL)), full((D_MODEL, 2 * B_WIDTH)),
                  full((CONV_B_WIDTH, B_WIDTH)), full((1, B_WIDTH)), full((1, B_WIDTH)), full((1, B_WIDTH)),
                  full((A_WIDTH, D_MODEL)), full((B_WIDTH, D_MODEL)), full((1, D_MODEL)),
                  full((D_MODEL, D_MODEL)), mem, mem, full((D_MODEL, D_MODEL)), full((1, D_MODEL))],
        out_specs=[tok(D_MODEL), tok(D_MODEL)],
        out_shape=[jax.ShapeDtypeStruct((T, D_MODEL), F32), jax.ShapeDtypeStruct((T, D_MODEL), BF16)],
        scratch_shapes=[
            pltpu.VMEM((CONV_HALO + tm, B_WIDTH), F32),
            pltpu.VMEM((tm, B_WIDTH), BF16),
        ],
        compiler_params=pltpu.CompilerParams(
            dimension_semantics=("arbitrary",), vmem_limit_bytes=VMEM_LIMIT_BYTES),
        name="mid",
    )(x2d, x2d, a, gm, wg, cw, cb, lng, lnb, wa, wb, gc, wq, kc, vc, wo, gf)


def _ffn_kernel(tiles_per_seq, x_ref, h_ref, wg_ref, wu_ref, cw_ref, cb_ref, wd_ref, gfin_ref, y_ref,
                halo, gbuf, act_ref):
    tm = x_ref.shape[0]

    @pl.when(pl.program_id(0) % tiles_per_seq == 0)
    def _():
        halo[...] = jnp.zeros(halo.shape, F32)

    h = h_ref[...]
    for c in range(D_FF // FF_CHUNK):
        cs = slice(c * FF_CHUNK, (c + 1) * FF_CHUNK)
        gb = gbuf.at[c % 2]
        gate = _dot(h, wg_ref[:, cs])
        up = _dot(h, wu_ref[:, cs])
        gb[0:SUBLANES, :] = halo[:, cs]
        gb[SUBLANES:SUBLANES + tm, :] = gate
        halo[:, cs] = gate[tm - SUBLANES:, :]
        cw = cw_ref[:, cs]
        g = cb_ref[:, cs] + cw[2:3, :] * gate
        g = g + cw[1:2, :] * gb[SUBLANES - 1:SUBLANES - 1 + tm, :]
        g = g + cw[0:1, :] * gb[SUBLANES - 2:SUBLANES - 2 + tm, :]
        act_ref[:, cs] = (g * jax.nn.sigmoid(g) * up).astype(BF16)
    y_ref[...] = _rmsnorm(x_ref[...] + _dot(act_ref[...], wd_ref[...]), gfin_ref[...])


def _ffn(x2, hf, wg, wu, cw, cb, wd, gfin, seq):
    T = x2.shape[0]
    tm = TOKEN_TILE
    full = lambda shape: pl.BlockSpec(shape, lambda i: (0,) * len(shape), pipeline_mode=pl.Buffered(1))
    tok = pl.BlockSpec((tm, D_MODEL), lambda i: (i, 0))
    return pl.pallas_call(
        functools.partial(_ffn_kernel, seq // tm),
        grid=(T // tm,),
        in_specs=[tok, tok, full((D_MODEL, D_FF)), full((D_MODEL, D_FF)), full((FFN_CONV_WIDTH, D_FF)),
                  full((1, D_FF)), full((D_FF, D_MODEL)), full((1, D_MODEL))],
        out_specs=tok,
        out_shape=jax.ShapeDtypeStruct((T, D_MODEL), F32),
        scratch_shapes=[
            pltpu.VMEM((SUBLANES, D_FF), F32),
            pltpu.VMEM((2, SUBLANES + tm, FF_CHUNK), F32),
            pltpu.VMEM((tm, D_FF), BF16),
        ],
        compiler_params=pltpu.CompilerParams(
            dimension_semantics=("arbitrary",), vmem_limit_bytes=VMEM_LIMIT_BYTES),
        name="ffn",
    )(x2, hf, wg, wu, cw, cb, wd, gfin)


def kernel(x, mem, positions, norm_mix_g, w_in, w_out, conv_b_w, conv_b_b, ln_b_g, ln_b_b, norm_cross_g, norm_mem_g, w_q_cross, w_k_cross, w_v_cross, w_o_cross, norm_ffn_g, w_gate, w_up, ffn_conv_w, ffn_conv_b, w_down, norm_final_g):
    batch, seq, _ = x.shape
    n_mem = mem.shape[1]
    depth = w_in.shape[0]
    assert depth == 1 and seq % TOKEN_TILE == 0 and seq % MID_TILE == 0 and seq % QUERY_TILE == 0
    assert seq // BF16_ROWS <= BF16_EXACT_INT

    row = lambda v: v.reshape(1, -1)
    x2d = x.reshape(batch * seq, D_MODEL)
    pos = positions.reshape(1, batch * seq)
    freq = (ROPE_THETA ** (-jnp.arange(0, ROT_DIM, 2, dtype=F32) / ROT_DIM)).reshape(ROT_HALF, 1)

    wi_ = w_in[0]
    c_q, c_k, c_v, c_qi, c_ki, c_wi = 0, A_WIDTH, A_WIDTH + 64, A_WIDTH + 128, A_WIDTH + 384, A_WIDTH + 448
    c_glu = c_wi + IDX_HEADS
    wt = jnp.concatenate([
        wi_[:, c_q:c_k], wi_[:, c_qi:c_ki], wi_[:, c_k:c_v], wi_[:, c_ki:c_wi], wi_[:, c_v:c_qi],
        wi_[:, c_wi:c_glu], jnp.zeros((D_MODEL, ROWS_T - ROW_WI - IDX_HEADS), F32)], axis=1).T.astype(BF16)
    wg = wi_[:, c_glu:].astype(BF16)

    qT, qiT, k, ki, vT, wiT = _in_proj(pos, x2d, row(norm_mix_g[0]), wt, freq)
    a_out = _dsa(qT, qiT, wiT, k, ki, vT, batch, seq)

    kc, vc = _mem_kv(mem.reshape(batch * n_mem, D_MODEL), row(norm_mem_g[0]),
                     w_k_cross[0].astype(BF16), w_v_cross[0].astype(BF16))
    wo_mix = w_out[0].astype(BF16)
    x2, hf = _mid(x2d, a_out, row(norm_mix_g[0]), wg, conv_b_w[0], row(conv_b_b[0]), row(ln_b_g[0]),
                  row(ln_b_b[0]), wo_mix[:A_WIDTH], wo_mix[A_WIDTH:], row(norm_cross_g[0]),
                  w_q_cross[0].astype(BF16), kc, vc, w_o_cross[0].astype(BF16), row(norm_ffn_g[0]), seq, n_mem)
    y = _ffn(x2, hf, w_gate[0].astype(BF16), w_up[0].astype(BF16), ffn_conv_w[0], row(ffn_conv_b[0]),
             w_down[0].astype(BF16), row(norm_final_g), seq)
    return y.reshape(batch, seq, D_MODEL)
```

```python
import functools

import jax
import jax.numpy as jnp
from jax import lax
from jax.experimental import pallas as pl
from jax.experimental.pallas import tpu as pltpu

F32 = jnp.float32
BF16 = jnp.bfloat16

D_MODEL = 1024
A_HEADS = 8
HEAD_DIM = 64
A_WIDTH = A_HEADS * HEAD_DIM
IDX_HEADS = 4
IDX_DIM = 64
TOPK_MAX = 256
B_WIDTH = D_MODEL - A_WIDTH
CONV_B_WIDTH = 31
ROPE_THETA = 500000.0
ROT_DIM = HEAD_DIM // 4
ROT_HALF = ROT_DIM // 2
CROSS_HEADS = 4
CROSS_HEAD_DIM = D_MODEL // CROSS_HEADS
D_FF = 2816
FFN_CONV_WIDTH = 3
EPS = 1e-6

SUBLANES = 8
LANES = 128
TOKEN_TILE = 512
MID_TILE = 1024
QUERY_TILE = 512
CONV_HALO = 32
CONV_ROWS = 64
SCORE_ROWS = 512
COUNT_ROWS = 8
BF16_ROWS = 16
BF16_EXACT_INT = 256
FF_CHUNK = 256
VMEM_LIMIT_BYTES = 56 * 1024 * 1024

ROW_Q = 0
ROW_QI = ROW_Q + A_WIDTH
ROW_K = ROW_QI + IDX_HEADS * IDX_DIM
ROW_KI = ROW_K + HEAD_DIM
ROW_V = ROW_KI + IDX_DIM
ROW_WI = ROW_V + HEAD_DIM
ROWS_T = ROW_WI + 16

V_ROWS = HEAD_DIM + 16
LOG2_E = 1.4426950408889634

INT_MIN = -(2 ** 31)
KEY_NEG_INF = INT_MIN + 0x7FFFFF
NEG_INF = float("-inf")


def _rmsnorm(x, g):
    ms = jnp.mean(x * x, axis=-1, keepdims=True)
    return (x * lax.rsqrt(ms + EPS)) * g


def _dot(a, b):
    return jnp.dot(a, b, preferred_element_type=F32)


def _dot_nt(a, b):
    return lax.dot_general(a, b, (((1,), (1,)), ((), ())), preferred_element_type=F32)


def _in_proj_kernel(pos_ref, x_ref, g_ref, wt_ref, freq_ref, qT_ref, qiT_ref, k_ref, ki_ref, vT_ref, wiT_ref):
    tm = x_ref.shape[0]
    h = _rmsnorm(x_ref[...], g_ref[...]).astype(BF16)

    r = _dot_nt(wt_ref[...], h)
    ang = freq_ref[...] * pos_ref[...].astype(F32)
    cos = jnp.cos(ang)
    sin = jnp.sin(ang)

    def rope(xt, heads):
        x3 = xt.reshape(heads, HEAD_DIM, tm)
        x1 = x3[:, 0:ROT_HALF, :]
        x2 = x3[:, ROT_HALF:ROT_DIM, :]
        out = jnp.concatenate([x1 * cos - x2 * sin, x2 * cos + x1 * sin, x3[:, ROT_DIM:, :]], axis=1)
        return out.reshape(heads * HEAD_DIM, tm)

    qT_ref[...] = (rope(r[ROW_Q:ROW_QI], A_HEADS) * (HEAD_DIM ** -0.5 * LOG2_E)).astype(BF16)
    qiT_ref[...] = (rope(r[ROW_QI:ROW_K], IDX_HEADS) * (IDX_DIM ** -0.5)).astype(BF16)
    kk = rope(r[ROW_K:ROW_V], 2).T
    k_ref[...] = kk[:, :HEAD_DIM].astype(BF16)
    ki_ref[...] = kk[:, HEAD_DIM:].astype(BF16)
    ones_rows = (lax.broadcasted_iota(jnp.int32, (V_ROWS - HEAD_DIM, tm), 0) == 0).astype(F32)
    vT_ref[...] = jnp.concatenate([r[ROW_V:ROW_WI], ones_rows], axis=0).astype(BF16)
    wiT_ref[...] = r[ROW_WI:ROW_WI + SUBLANES] * (IDX_HEADS ** -0.5)


def _group_b_steps(x, g_ref, wg_ref, cw_ref, cb_ref, lng_ref, lnb_ref, ubuf, b_buf):
    tm = x.shape[0]
    h = _rmsnorm(x, g_ref[...]).astype(BF16)
    glu = _dot(h, wg_ref[...])
    u = glu[:, :B_WIDTH] * jax.nn.sigmoid(glu[:, B_WIDTH:])
    ubuf[CONV_HALO:CONV_HALO + tm, :] = u
    base = CONV_HALO - (CONV_B_WIDTH - 1)
    n_chunks = tm // CONV_ROWS

    def conv_half(c, ls):
        c0 = c * CONV_ROWS
        acc = jnp.broadcast_to(cb_ref[:, ls], (CONV_ROWS, B_WIDTH // 2))
        for r in range(SUBLANES):
            rows = CONV_ROWS + (SUBLANES if r else 0)
            q = None
            for j in range(CONV_B_WIDTH):
                if (base + j) % SUBLANES != r:
                    continue
                a0 = c0 + (base + j) // SUBLANES * SUBLANES
                term = ubuf[a0:a0 + rows, ls] * cw_ref[j:j + 1, ls]
                q = term if q is None else q + term
            acc = acc + q[r:r + CONV_ROWS, :]
        return acc

    pending = []

    def first_half(c):
        pending.append(conv_half(c, slice(0, B_WIDTH // 2)))

    def second_half(c):
        acc = jnp.concatenate([pending.pop(), conv_half(c, slice(B_WIDTH // 2, B_WIDTH))], axis=1)
        mu = jnp.mean(acc, axis=-1, keepdims=True)
        var = jnp.mean(jnp.square(acc - mu), axis=-1, keepdims=True)
        y = (acc - mu) * lax.rsqrt(var + EPS) * lng_ref[...] + lnb_ref[...]
        b_buf[c * CONV_ROWS:(c + 1) * CONV_ROWS, :] = (y * jax.nn.sigmoid(y)).astype(BF16)
        if c == n_chunks - 1:
            ubuf[0:CONV_HALO, :] = ubuf[tm:tm + CONV_HALO, :]

    return [functools.partial(f, c) for c in range(n_chunks) for f in (first_half, second_half)]


def _in_proj(pos, x2d, g, wt, freq):
    T = x2d.shape[0]
    tm = TOKEN_TILE
    full = lambda shape: pl.BlockSpec(shape, lambda i: (0,) * len(shape))
    return pl.pallas_call(
        _in_proj_kernel,
        grid=(T // tm,),
        in_specs=[
            pl.BlockSpec((1, tm), lambda i: (0, i)),
            pl.BlockSpec((tm, D_MODEL), lambda i: (i, 0)),
            full((1, D_MODEL)),
            full((ROWS_T, D_MODEL)),
            full((ROT_HALF, 1)),
        ],
        out_specs=[
            pl.BlockSpec((A_WIDTH, tm), lambda i: (0, i)),
            pl.BlockSpec((IDX_HEADS * IDX_DIM, tm), lambda i: (0, i)),
            pl.BlockSpec((tm, HEAD_DIM), lambda i: (i, 0)),
            pl.BlockSpec((tm, IDX_DIM), lambda i: (i, 0)),
            pl.BlockSpec((V_ROWS, tm), lambda i: (0, i)),
            pl.BlockSpec((SUBLANES, tm), lambda i: (0, i)),
        ],
        out_shape=[
            jax.ShapeDtypeStruct((A_WIDTH, T), BF16),
            jax.ShapeDtypeStruct((IDX_HEADS * IDX_DIM, T), BF16),
            jax.ShapeDtypeStruct((T, HEAD_DIM), BF16),
            jax.ShapeDtypeStruct((T, IDX_DIM), BF16),
            jax.ShapeDtypeStruct((V_ROWS, T), BF16),
            jax.ShapeDtypeStruct((SUBLANES, T), F32),
        ],
        compiler_params=pltpu.CompilerParams(
            dimension_semantics=("arbitrary",), vmem_limit_bytes=VMEM_LIMIT_BYTES),
        name="in_proj",
    )(pos, x2d, g, wt, freq)


def _key_to_f32(key):
    bits = jnp.where(key >= 0, key, key ^ jnp.int32(0x7FFFFFFF))
    return lax.bitcast_convert_type(bits, F32)


def _high_half(x):
    bits = lax.bitcast_convert_type(x, jnp.int32) & jnp.int32(-(2 ** 16))
    return lax.bitcast_convert_type(bits, F32).astype(BF16)


def _dsa_kernel(qT_ref, qiT_ref, wT_ref, k_ref, ki_ref, vT_ref, o_ref, sc_ref, sh_ref, lg_ref, acc_ref, m_ref,
                cm_ref):
    tq = qT_ref.shape[1]
    kc = tq
    qi_blk = pl.program_id(1)
    nchunk = qi_blk + 1

    row = lax.broadcasted_iota(jnp.int32, (kc, tq), 0)
    col = lax.broadcasted_iota(jnp.int32, (kc, tq), 1)
    causal = row <= col

    def chunk_start(c):
        return c * kc if isinstance(c, int) else pl.multiple_of(c * kc, kc)

    w = wT_ref[...]

    def score_body(c, carry):
        r0 = pl.multiple_of(c * SCORE_ROWS, SCORE_ROWS)
        ki_c = ki_ref[pl.ds(r0, SCORE_ROWS), :]
        for ls in (slice(0, tq // 2), slice(tq // 2, tq)):
            s = jnp.zeros((SCORE_ROWS, tq // 2), F32)
            for h in range(IDX_HEADS):
                d = _dot(ki_c, qiT_ref[h * IDX_DIM:(h + 1) * IDX_DIM, ls])
                s = s + w[h:h + 1, ls] * jnp.maximum(d, 0.0)
            sc_ref[pl.ds(r0, SCORE_ROWS), ls] = s
            sh_ref[pl.ds(r0, SCORE_ROWS), ls] = _high_half(s)
        return carry

    lax.fori_loop(0, nchunk * (kc // SCORE_ROWS), score_body, 0)
    d0 = chunk_start(qi_blk)
    diag = jnp.where(causal, sc_ref[pl.ds(d0, kc), :], NEG_INF)
    sc_ref[pl.ds(d0, kc), :] = diag
    sh_ref[pl.ds(d0, kc), :] = _high_half(diag)

    lane_tiles = [slice(j * LANES, (j + 1) * LANES) for j in range(tq // LANES)]

    def count_ge16(thr):
        one = jnp.ones((), BF16)
        zero = jnp.zeros((), BF16)

        def body(c, acc):
            r0 = chunk_start(c)
            for g in range(kc // BF16_ROWS):
                blk = sh_ref[pl.ds(r0 + g * BF16_ROWS, BF16_ROWS), :]
                acc = acc + jnp.where(blk >= thr, one, zero)
            return acc
        acc = lax.fori_loop(0, nchunk, body, jnp.zeros((BF16_ROWS, tq), BF16))
        return acc.astype(F32).sum(axis=0, keepdims=True)

    def bit16_body(it, t):
        trial = t + lax.shift_left(jnp.int32(1), 15 - it)
        bits = jnp.where(trial >= 0, trial, trial ^ jnp.int32(0x7FFF))
        thr = lax.bitcast_convert_type(lax.shift_left(bits, 16), F32).astype(BF16)
        return jnp.where(count_ge16(thr) >= TOPK_MAX, trial, t)

    t16 = lax.fori_loop(0, 16, bit16_body, jnp.full((1, tq), -(2 ** 15), jnp.int32))

    def count_ge(thr):
        def body(c, acc):
            blk = sc_ref[pl.ds(chunk_start(c), kc), :]
            ge = (blk >= thr).astype(jnp.int32)
            return acc + ge.reshape(kc // COUNT_ROWS, COUNT_ROWS, tq).sum(axis=0)
        acc = lax.fori_loop(0, nchunk - 1, body, jnp.zeros((COUNT_ROWS, tq), jnp.int32))
        parts = []
        for j, ls in enumerate(lane_tiles):
            rows = (j + 1) * LANES
            ge = (sc_ref[pl.ds(d0, rows), ls] >= thr[:, ls]).astype(jnp.int32)
            parts.append(acc[:, ls] + ge.reshape(rows // COUNT_ROWS, COUNT_ROWS, LANES).sum(axis=0))
        acc = jnp.concatenate(parts, axis=1)
        return acc.sum(axis=0, keepdims=True)

    def bit_body(it, t):
        trial = t + lax.shift_left(jnp.int32(1), 15 - it)
        cnt = count_ge(_key_to_f32(trial))
        return jnp.where(cnt >= TOPK_MAX, trial, t)

    t = lax.fori_loop(0, 16, bit_body, lax.shift_left(t16, 16))
    tau = _key_to_f32(jnp.maximum(t, KEY_NEG_INF))

    def gt_body(c, acc):
        blk = sc_ref[pl.ds(chunk_start(c), kc), :]
        return acc + (blk > tau).astype(jnp.int32).reshape(kc // COUNT_ROWS, COUNT_ROWS, tq).sum(axis=0)

    n_gt = lax.fori_loop(0, nchunk, gt_body, jnp.zeros((COUNT_ROWS, tq), jnp.int32)).sum(axis=0, keepdims=True)
    need = (TOPK_MAX - n_gt).astype(F32)

    lower_tri = (row > col).astype(BF16)

    def bias_body(c, carry):
        r0 = chunk_start(c)
        counts = []
        for ls in (slice(0, tq // 2), slice(tq // 2, tq)):
            blk = sc_ref[pl.ds(r0, kc), ls]
            eq = blk == tau[:, ls]
            eqf = jnp.where(eq, 1.0, 0.0)
            rank = _dot(lower_tri, eqf.astype(BF16)) + carry[:, ls]
            sel = (blk > tau[:, ls]) | (eq & (rank < need[:, ls]))
            sc_ref[pl.ds(r0, kc), ls] = jnp.where(sel, 0.0, NEG_INF)
            counts.append(eqf.sum(axis=0, keepdims=True))
        return carry + jnp.concatenate(counts, axis=1)

    lax.fori_loop(0, nchunk, bias_body, jnp.zeros((1, tq), F32))
    sc_ref[pl.ds(d0, kc), :] = jnp.where(causal, sc_ref[pl.ds(d0, kc), :], NEG_INF)

    m_ref[...] = jnp.full(m_ref.shape, NEG_INF, F32)
    acc_ref[...] = jnp.zeros(acc_ref.shape, F32)
    units = [(h, ls) for h in range(A_HEADS) for ls in (slice(0, tq // 2), slice(tq // 2, tq))]

    def stage_chunk(c, consume_unit):
        r0 = chunk_start(c)
        k_c = k_ref[pl.ds(r0, kc), :]
        bias = sc_ref[pl.ds(r0, kc), :]
        cmax = [[], []]
        for u, (h, ls) in enumerate(units):
            consume_unit(h, ls)
            logit = _dot(k_c, qT_ref[h * HEAD_DIM:(h + 1) * HEAD_DIM, ls]) + bias[:, ls]
            lg_ref[h, :, ls] = logit
            cmax[u % 2].append(logit.max(axis=0, keepdims=True))
        cm_ref[...] = jnp.concatenate([jnp.concatenate(half, axis=0) for half in cmax], axis=1)

    def consumer(c):
        vT_c = vT_ref[:, pl.ds(chunk_start(c), kc)]
        m_old = m_ref[...]
        m_new = jnp.maximum(m_old, cm_ref[...])
        m_safe = jnp.where(m_new == NEG_INF, 0.0, m_new)
        alpha = jnp.exp2(m_old - m_safe)
        m_ref[...] = m_new

        def consume_unit(h, ls):
            p = jnp.exp2(lg_ref[h, :, ls] - m_safe[h:h + 1, ls]).astype(BF16)
            acc_ref[h, :, ls] = alpha[h:h + 1, ls] * acc_ref[h, :, ls] + _dot(vT_c, p)
        return consume_unit

    stage_chunk(0, lambda h, ls: None)

    def att_body(c, carry):
        stage_chunk(c, consumer(c - 1))
        return carry

    lax.fori_loop(1, nchunk, att_body, 0)
    last = consumer(nchunk - 1)
    for h, ls in units:
        last(h, ls)

    outs = []
    for h in range(A_HEADS):
        a = acc_ref[h]
        outs.append(a[:HEAD_DIM, :] * (1.0 / a[HEAD_DIM:HEAD_DIM + 1, :]))
    o_ref[...] = jnp.concatenate(outs, axis=0).T.astype(BF16)


def _dsa(qT, qiT, wiT, k, ki, vT, batch, seq):
    tq = QUERY_TILE
    nq = seq // tq
    T = batch * seq
    return pl.pallas_call(
        _dsa_kernel,
        grid=(batch, nq),
        in_specs=[
            pl.BlockSpec((A_WIDTH, tq), lambda b, i: (0, b * nq + i)),
            pl.BlockSpec((IDX_HEADS * IDX_DIM, tq), lambda b, i: (0, b * nq + i)),
            pl.BlockSpec((SUBLANES, tq), lambda b, i: (0, b * nq + i)),
            pl.BlockSpec((seq, HEAD_DIM), lambda b, i: (b, 0)),
            pl.BlockSpec((seq, IDX_DIM), lambda b, i: (b, 0)),
            pl.BlockSpec((V_ROWS, seq), lambda b, i: (0, b)),
        ],
        out_specs=pl.BlockSpec((tq, A_WIDTH), lambda b, i: (b * nq + i, 0)),
        out_shape=jax.ShapeDtypeStruct((T, A_WIDTH), BF16),
        scratch_shapes=[
            pltpu.VMEM((seq, tq), F32),
            pltpu.VMEM((seq, tq), BF16),
            pltpu.VMEM((A_HEADS, tq, tq), F32),
            pltpu.VMEM((A_HEADS, V_ROWS, tq), F32),
            pltpu.VMEM((A_HEADS, tq), F32),
            pltpu.VMEM((A_HEADS, tq), F32),
        ],
        compiler_params=pltpu.CompilerParams(
            dimension_semantics=("arbitrary", "arbitrary"), vmem_limit_bytes=VMEM_LIMIT_BYTES),
        name="dsa",
    )(qT, qiT, wiT, k, ki, vT)


def _mem_kv_kernel(mem_ref, g_ref, wk_ref, wv_ref, kc_ref, vc_ref):
    m = _rmsnorm(mem_ref[...], g_ref[...]).astype(BF16)
    kc_ref[...] = _dot(m, wk_ref[...]).astype(BF16)
    vc_ref[...] = _dot(m, wv_ref[...]).astype(BF16)


def _mem_kv(mem2d, g, wk, wv):
    M = mem2d.shape[0]
    tm = TOKEN_TILE
    full = lambda shape: pl.BlockSpec(shape, lambda i: (0,) * len(shape))
    return pl.pallas_call(
        _mem_kv_kernel,
        grid=(M // tm,),
        in_specs=[pl.BlockSpec((tm, D_MODEL), lambda i: (i, 0)), full((1, D_MODEL)),
                  full((D_MODEL, D_MODEL)), full((D_MODEL, D_MODEL))],
        out_specs=[pl.BlockSpec((tm, D_MODEL), lambda i: (i, 0))] * 2,
        out_shape=[jax.ShapeDtypeStruct((M, D_MODEL), BF16)] * 2,
        compiler_params=pltpu.CompilerParams(
            dimension_semantics=("arbitrary",), vmem_limit_bytes=VMEM_LIMIT_BYTES),
        name="mem_kv",
    )(mem2d, g, wk, wv)


def _mid_kernel(tiles_per_seq, xn_ref, x_ref, a_ref, gm_ref, wg_ref, cw_ref, cb_ref, lng_ref, lnb_ref, wa_ref,
                wb_ref, gc_ref, wq_ref, kc_ref, vc_ref, wo_ref, gf_ref, x2_ref, hf_ref, ubuf, b_buf):
    step = pl.program_id(0)

    @pl.when(step == 0)
    def _():
        b_buf[...] = jnp.zeros(b_buf.shape, BF16)

    @pl.when(step % tiles_per_seq == 0)
    def _():
        ubuf[0:CONV_HALO, :] = jnp.zeros((CONV_HALO, B_WIDTH), F32)

    x = x_ref[...]
    a = a_ref[...]
    b = b_buf[...]
    conv_chunks = _group_b_steps(xn_ref[...], gm_ref, wg_ref, cw_ref, cb_ref, lng_ref, lnb_ref, ubuf, b_buf)
    stages_left = [6 * CROSS_HEADS]

    def run_conv_chunks():
        for _ in range(-(-len(conv_chunks) // stages_left[0])):
            conv_chunks.pop(0)()
        stages_left[0] -= 1

    head_cols = [slice(h * CROSS_HEAD_DIM, (h + 1) * CROSS_HEAD_DIM) for h in range(CROSS_HEADS)]
    x1_cols = []
    for cs in head_cols:
        part = x[:, cs] + _dot(a, wa_ref[:, cs])
        run_conv_chunks()
        x1_cols.append(part + _dot(b, wb_ref[:, cs]))
        run_conv_chunks()
    x1 = jnp.concatenate(x1_cols, axis=-1)
    hq = _rmsnorm(x1, gc_ref[...]).astype(BF16)
    heads = []
    for hs in head_cols:
        qc = (_dot(hq, wq_ref[:, hs]) * (CROSS_HEAD_DIM ** -0.5)).astype(BF16)
        run_conv_chunks()
        logit = _dot_nt(qc, kc_ref[:, hs])
        p = jnp.exp(logit - logit.max(axis=-1, keepdims=True))
        p = p * (1.0 / p.sum(axis=-1, keepdims=True))
        run_conv_chunks()
        heads.append(_dot(p.astype(BF16), vc_ref[:, hs]))
        run_conv_chunks()
    oc = jnp.concatenate(heads, axis=-1).astype(BF16)
    x2_cols = []
    for cs in head_cols:
        x2_cols.append(x1[:, cs] + _dot(oc, wo_ref[:, cs]))
        run_conv_chunks()
    assert not conv_chunks
    x2 = jnp.concatenate(x2_cols, axis=-1)
    x2_ref[...] = x2
    hf_ref[...] = _rmsnorm(x2, gf_ref[...]).astype(BF16)


def _mid(x2d, a, gm, wg, cw, cb, lng, lnb, wa, wb, gc, wq, kc, vc, wo, gf, seq, n_mem):
    T = x2d.shape[0]
    tm = MID_TILE
    n_tiles = T // tm
    tiles_per_seq = seq // tm
    full = lambda shape: pl.BlockSpec(shape, lambda i: (0,) * len(shape), pipeline_mode=pl.Buffered(1))
    prev = lambda i: jnp.maximum(i - 1, 0)
    tok = lambda width: pl.BlockSpec((tm, width), lambda i: (prev(i), 0))
    mem = pl.BlockSpec((n_mem, D_MODEL), lambda i: (prev(i) // tiles_per_seq, 0))
    return pl.pallas_call(
        functools.partial(_mid_kernel, tiles_per_seq),
        grid=(n_tiles + 1,),
        in_specs=[pl.BlockSpec((tm, D_MODEL), lambda i: (jnp.minimum(i, n_tiles - 1), 0)),
                  tok(D_MODEL), tok(A_WIDTH), full((1, D_MODEL)), full((D_MODEL, 2 * B_WIDTH)),
                  full((CONV_B_WIDTH, B_WIDTH)), full((1, B_WIDTH)), full((1, B_WIDTH)), full((1, B_WIDTH)),
                  full((A_WIDTH, D_MODEL)), full((B_WIDTH, D_MODEL)), full((1, D_MODEL)),
                  full((D_MODEL, D_MODEL)), mem, mem, full((D_MODEL, D_MODEL)), full((1, D_MODEL))],
        out_specs=[tok(D_MODEL), tok(D_MODEL)],
        out_shape=[jax.ShapeDtypeStruct((T, D_MODEL), F32), jax.ShapeDtypeStruct((T, D_MODEL), BF16)],
        scratch_shapes=[
            pltpu.VMEM((CONV_HALO + tm, B_WIDTH), F32),
            pltpu.VMEM((tm, B_WIDTH), BF16),
        ],
        compiler_params=pltpu.CompilerParams(
            dimension_semantics=("arbitrary",), vmem_limit_bytes=VMEM_LIMIT_BYTES),
        name="mid",
    )(x2d, x2d, a, gm, wg, cw, cb, lng, lnb, wa, wb, gc, wq, kc, vc, wo, gf)


def _ffn_kernel(tiles_per_seq, x_ref, h_ref, wg_ref, wu_ref, cw_ref, cb_ref, wd_ref, gfin_ref, y_ref,
                halo, gbuf, act_ref):
    tm = x_ref.shape[0]

    @pl.when(pl.program_id(0) % tiles_per_seq == 0)
    def _():
        halo[...] = jnp.zeros(halo.shape, F32)

    h = h_ref[...]
    for c in range(D_FF // FF_CHUNK):
        cs = slice(c * FF_CHUNK, (c + 1) * FF_CHUNK)
        gb = gbuf.at[c % 2]
        gate = _dot(h, wg_ref[:, cs])
        up = _dot(h, wu_ref[:, cs])
        gb[0:SUBLANES, :] = halo[:, cs]
        gb[SUBLANES:SUBLANES + tm, :] = gate
        halo[:, cs] = gate[tm - SUBLANES:, :]
        cw = cw_ref[:, cs]
        g = cb_ref[:, cs] + cw[2:3, :] * gate
        g = g + cw[1:2, :] * gb[SUBLANES - 1:SUBLANES - 1 + tm, :]
        g = g + cw[0:1, :] * gb[SUBLANES - 2:SUBLANES - 2 + tm, :]
        act_ref[:, cs] = (g * jax.nn.sigmoid(g) * up).astype(BF16)
    y_ref[...] = _rmsnorm(x_ref[...] + _dot(act_ref[...], wd_ref[...]), gfin_ref[...])


def _ffn(x2, hf, wg, wu, cw, cb, wd, gfin, seq):
    T = x2.shape[0]
    tm = TOKEN_TILE
    full = lambda shape: pl.BlockSpec(shape, lambda i: (0,) * len(shape), pipeline_mode=pl.Buffered(1))
    tok = pl.BlockSpec((tm, D_MODEL), lambda i: (i, 0))
    return pl.pallas_call(
        functools.partial(_ffn_kernel, seq // tm),
        grid=(T // tm,),
        in_specs=[tok, tok, full((D_MODEL, D_FF)), full((D_MODEL, D_FF)), full((FFN_CONV_WIDTH, D_FF)),
                  full((1, D_FF)), full((D_FF, D_MODEL)), full((1, D_MODEL))],
        out_specs=tok,
        out_shape=jax.ShapeDtypeStruct((T, D_MODEL), F32),
        scratch_shapes=[
            pltpu.VMEM((SUBLANES, D_FF), F32),
            pltpu.VMEM((2, SUBLANES + tm, FF_CHUNK), F32),
            pltpu.VMEM((tm, D_FF), BF16),
        ],
        compiler_params=pltpu.CompilerParams(
            dimension_semantics=("arbitrary",), vmem_limit_bytes=VMEM_LIMIT_BYTES,
            allow_input_fusion=[False, False, True, True, False, False, True, False]),
        name="ffn",
    )(x2, hf, wg, wu, cw, cb, wd, gfin)


def kernel(x, mem, positions, norm_mix_g, w_in, w_out, conv_b_w, conv_b_b, ln_b_g, ln_b_b, norm_cross_g, norm_mem_g, w_q_cross, w_k_cross, w_v_cross, w_o_cross, norm_ffn_g, w_gate, w_up, ffn_conv_w, ffn_conv_b, w_down, norm_final_g):
    batch, seq, _ = x.shape
    n_mem = mem.shape[1]
    depth = w_in.shape[0]
    assert depth == 1 and seq % TOKEN_TILE == 0 and seq % MID_TILE == 0 and seq % QUERY_TILE == 0
    assert seq // BF16_ROWS <= BF16_EXACT_INT

    row = lambda v: v.reshape(1, -1)
    x2d = x.reshape(batch * seq, D_MODEL)
    pos = positions.reshape(1, batch * seq)
    freq = (ROPE_THETA ** (-jnp.arange(0, ROT_DIM, 2, dtype=F32) / ROT_DIM)).reshape(ROT_HALF, 1)

    wi_ = w_in[0]
    c_q, c_k, c_v, c_qi, c_ki, c_wi = 0, A_WIDTH, A_WIDTH + 64, A_WIDTH + 128, A_WIDTH + 384, A_WIDTH + 448
    c_glu = c_wi + IDX_HEADS
    wt = jnp.concatenate([
        wi_[:, c_q:c_k], wi_[:, c_qi:c_ki], wi_[:, c_k:c_v], wi_[:, c_ki:c_wi], wi_[:, c_v:c_qi],
        wi_[:, c_wi:c_glu], jnp.zeros((D_MODEL, ROWS_T - ROW_WI - IDX_HEADS), F32)], axis=1).T.astype(BF16)
    wg = wi_[:, c_glu:].astype(BF16)

    qT, qiT, k, ki, vT, wiT = _in_proj(pos, x2d, row(norm_mix_g[0]), wt, freq)
    a_out = _dsa(qT, qiT, wiT, k, ki, vT, batch, seq)

    kc, vc = _mem_kv(mem.reshape(batch * n_mem, D_MODEL), row(norm_mem_g[0]),
                     w_k_cross[0].astype(BF16), w_v_cross[0].astype(BF16))
    wo_mix = w_out[0].astype(BF16)
    x2, hf = _mid(x2d, a_out, row(norm_mix_g[0]), wg, conv_b_w[0], row(conv_b_b[0]), row(ln_b_g[0]),
                  row(ln_b_b[0]), wo_mix[:A_WIDTH], wo_mix[A_WIDTH:], row(norm_cross_g[0]),
                  w_q_cross[0].astype(BF16), kc, vc, w_o_cross[0].astype(BF16), row(norm_ffn_g[0]), seq, n_mem)
    y = _ffn(x2, hf, w_gate[0].astype(BF16), w_up[0].astype(BF16), ffn_conv_w[0], row(ffn_conv_b[0]),
             w_down[0].astype(BF16), row(norm_final_g), seq)
    return y.reshape(batch, seq, D_MODEL)
```
